```python
import math
import jax
import jax.numpy as jnp
from jax import lax
import numpy as np

D_MODEL = 1024
BATCH = 4
SEQ = 4096
DEPTH = 2
DEC_BATCH = 32
DEC_SEQ = 4
PAST_LEN = 16384
PAGE_SIZE = 128

N_EVEN = (DEPTH + 1) // 2
N_ODD = DEPTH // 2
S5_WIDTH = D_MODEL // 2
S5_GROUP = 16
S5_GROUPS = S5_WIDTH // S5_GROUP
S5_STATE = 64
GLA_HEADS = 4
GLA_DV = (D_MODEL // 2) // GLA_HEADS
GLA_DK = GLA_DV // 2
GLA_RANK = 16
GLA_GATE_NORM = 16.0
GLA_CHUNK = 16
SB_HEADS = 16
SB_DIM = D_MODEL // SB_HEADS
SB_BLOCK = 128
SB_BIAS_INIT = -8.0
D_FF = 2816
CONV_W = 3
IN0_WIDTH = S5_WIDTH + GLA_HEADS * (2 * GLA_DK + 2 * GLA_DV) + GLA_RANK
EPS = 1e-6

kernel_name = 'hybrid_s5_gla_stickbreak_convffn_step'


def rmsnorm(x, g):
    x32 = x.astype(jnp.float32)
    y = x32 * lax.rsqrt(jnp.mean(x32 * x32, axis=-1, keepdims=True) + EPS)
    return (y * g.astype(jnp.float32)).astype(x.dtype)


def _complex_affine_combine(e1, e2):
    a1r, a1i, b1r, b1i = e1
    a2r, a2i, b2r, b2i = e2
    return (a2r * a1r - a2i * a1i,
            a2r * a1i + a2i * a1r,
            a2r * b1r - a2i * b1i + b2r,
            a2r * b1i + a2i * b1r + b2i)


def s5_mixer(u, h0_re, h0_im, a_re, a_im, log_dt, b_re, b_im, c_re, c_im, d_skip, w_glu, b_glu):
    f32 = jnp.float32
    bsz, s, _ = u.shape
    u32 = u.astype(f32)
    ug = u32.reshape(bsz, s, S5_GROUPS, S5_GROUP)
    ar, ai = a_re.astype(f32), a_im.astype(f32)
    dt = jnp.exp(log_dt.astype(f32))[:, None]
    mag = jnp.exp(ar * dt)
    abar_re, abar_im = mag * jnp.cos(ai * dt), mag * jnp.sin(ai * dt)
    den = ar * ar + ai * ai
    nr, ni = abar_re - 1.0, abar_im
    coef_re = (nr * ar + ni * ai) / den
    coef_im = (ni * ar - nr * ai) / den
    br, bi = b_re.astype(f32), b_im.astype(f32)
    bbar_re = coef_re[..., None] * br - coef_im[..., None] * bi
    bbar_im = coef_re[..., None] * bi + coef_im[..., None] * br
    bu_re = jnp.einsum('bsgh,gph->bsgp', ug, bbar_re)
    bu_im = jnp.einsum('bsgh,gph->bsgp', ug, bbar_im)
    a_r = jnp.broadcast_to(abar_re, bu_re.shape)
    a_i = jnp.broadcast_to(abar_im, bu_re.shape)
    cum_r, cum_i, hr, hi = lax.associative_scan(_complex_affine_combine, (a_r, a_i, bu_re, bu_im), axis=1)
    h0r = h0_re.astype(f32)[:, None]
    h0i = h0_im.astype(f32)[:, None]
    h_re = hr + cum_r * h0r - cum_i * h0i
    h_im = hi + cum_r * h0i + cum_i * h0r
    y = (jnp.einsum('bsgp,ghp->bsgh', h_re, c_re.astype(f32))
         - jnp.einsum('bsgp,ghp->bsgh', h_im, c_im.astype(f32)))
    y = y.reshape(bsz, s, S5_WIDTH) + d_skip.astype(f32) * u32
    z = jax.nn.gelu(y)
    out = z * jax.nn.sigmoid(z @ w_glu.astype(f32) + b_glu.astype(f32))
    return out, h_re[:, -1], h_im[:, -1]


def gla_chunked(q, k, v, gk, s0):
    bsz, s, h, dk = q.shape
    dv = v.shape[-1]
    c = GLA_CHUNK if s % GLA_CHUNK == 0 else s
    n = s // c

    def blocks(t):
        return t.reshape(bsz, n, c, h, t.shape[-1]).transpose(1, 0, 3, 2, 4)

    qc, kc, vc, gc = blocks(q), blocks(k), blocks(v), blocks(gk)
    bcum = jnp.cumsum(gc, axis=3)
    incl = jnp.tril(jnp.ones((c, c), dtype=bool))
    diff = jnp.where(incl[:, :, None], bcum[..., :, None, :] - bcum[..., None, :, :], -jnp.inf)
    att = jnp.einsum('nbhtd,nbhsd,nbhtsd->nbhts', qc, kc, jnp.exp(diff))

    def step(state, blk):
        qb, kb, vb, bb, ab = blk
        blast = bb[:, :, -1, :]
        o = (jnp.einsum('bhtd,bhdv->bhtv', qb * jnp.exp(bb), state)
             + jnp.einsum('bhts,bhsv->bhtv', ab, vb))
        state = (jnp.exp(blast)[..., None] * state
                 + jnp.einsum('bhsd,bhsv->bhdv', kb * jnp.exp(blast[:, :, None, :] - bb), vb))
        return state, o

    s_fin, o = lax.scan(step, s0.astype(jnp.float32), (qc, kc, vc, bcum, att))
    return o.transpose(1, 0, 3, 2, 4).reshape(bsz, s, h, dv), s_fin


def even_mixer(h, s5_h0_re, s5_h0_im, gla_s0, w_in, a_re, a_im, log_dt, b_re, b_im, c_re, c_im,
               d_skip, w_glu, b_glu, w_gk, b_gk, gla_norm, w_out):
    f32 = jnp.float32
    bsz, s, _ = h.shape
    p = h @ w_in
    o1 = S5_WIDTH
    o2 = o1 + GLA_HEADS * GLA_DK
    o3 = o2 + GLA_HEADS * GLA_DK
    o4 = o3 + GLA_HEADS * GLA_DV
    o5 = o4 + GLA_HEADS * GLA_DV
    u, q, k, v, g, r = p[..., :o1], p[..., o1:o2], p[..., o2:o3], p[..., o3:o4], p[..., o4:o5], p[..., o5:]
    y_s5, hr, hi = s5_mixer(u, s5_h0_re, s5_h0_im, a_re, a_im, log_dt, b_re, b_im, c_re, c_im, d_skip, w_glu, b_glu)
    gk = jax.nn.log_sigmoid((r @ w_gk + b_gk).astype(f32)) / GLA_GATE_NORM
    qh = q.astype(f32).reshape(bsz, s, GLA_HEADS, GLA_DK) * (GLA_DK ** -0.5)
    kh = k.astype(f32).reshape(bsz, s, GLA_HEADS, GLA_DK)
    vh = v.astype(f32).reshape(bsz, s, GLA_HEADS, GLA_DV)
    gkh = gk.reshape(bsz, s, GLA_HEADS, GLA_DK)
    o, s_fin = gla_chunked(qh, kh, vh, gkh, gla_s0)
    o = o * lax.rsqrt(jnp.mean(o * o, axis=-1, keepdims=True) + EPS) * gla_norm.astype(f32)
    o = o.reshape(bsz, s, GLA_HEADS * GLA_DV) * jax.nn.silu(g.astype(f32))
    y = jnp.concatenate([y_s5, o], axis=-1).astype(h.dtype) @ w_out
    return y, hr, hi, s_fin


def sb_qkv(h, w_qkv):
    bsz, s, _ = h.shape
    qkv = (h @ w_qkv).reshape(bsz, s, 3, SB_HEADS, SB_DIM)
    return qkv[:, :, 0], qkv[:, :, 1], qkv[:, :, 2]


def sb_prompt(q, k, v, bias):
    f32 = jnp.float32
    bsz, s, h, d = q.shape
    nb = s // SB_BLOCK
    kt = k.astype(f32).transpose(0, 2, 1, 3)
    vt = v.astype(f32).transpose(0, 2, 1, 3)
    qb = q.astype(f32).reshape(bsz, nb, SB_BLOCK, h, d).transpose(1, 0, 3, 2, 4)
    key_pos = jnp.arange(s)
    bh = bias.astype(f32)[None, :, None, None]

    def block(args):
        qblk, idx = args
        qpos = idx * SB_BLOCK + jnp.arange(SB_BLOCK)
        z = jnp.einsum('bhtd,bhsd->bhts', qblk, kt) * (SB_DIM ** -0.5) + bh
        mask = key_pos[None, :] < qpos[:, None]
        lo = jnp.where(mask, jax.nn.log_sigmoid(-z), 0.0)
        surv = lax.cumsum(lo, axis=3, reverse=True) - lo
        a = jnp.where(mask, jnp.exp(jax.nn.log_sigmoid(z) + surv), 0.0)
        return jnp.einsum('bhts,bhsd->bhtd', a, vt)

    o = lax.map(block, (qb, jnp.arange(nb)))
    return o.transpose(1, 0, 3, 2, 4).reshape(bsz, s, h * d)


def sb_sample(q, k_new, v_new, cache_k, cache_v, layer, page_table, bias):
    f32 = jnp.float32
    bsz, nq, h, d = q.shape
    scale = SB_DIM ** -0.5
    bh = bias.astype(f32)[None, :, None, None]
    qh = q.astype(f32).transpose(0, 2, 1, 3)
    kn = k_new.astype(f32).transpose(0, 2, 1, 3)
    vn = v_new.astype(f32).transpose(0, 2, 1, 3)
    z = jnp.einsum('bhtd,bhsd->bhts', qh, kn) * scale + bh
    mask = jnp.tril(jnp.ones((nq, nq), dtype=bool), k=-1)
    lo = jnp.where(mask, jax.nn.log_sigmoid(-z), 0.0)
    surv = lax.cumsum(lo, axis=3, reverse=True) - lo
    a = jnp.where(mask, jnp.exp(jax.nn.log_sigmoid(z) + surv), 0.0)
    out = jnp.einsum('bhts,bhsd->bhtd', a, vn)
    carry = jnp.sum(lo, axis=-1)

    def page_step(c, pt):
        acc, sv = c
        kp = cache_k[layer, pt].astype(f32)
        vp = cache_v[layer, pt].astype(f32)
        zp = jnp.einsum('bhtd,bshd->bhts', qh, kp) * scale + bh
        lop = jax.nn.log_sigmoid(-zp)
        sp = lax.cumsum(lop, axis=3, reverse=True) - lop
        ap = jnp.exp(jax.nn.log_sigmoid(zp) + sp + sv[..., None])
        acc = acc + jnp.einsum('bhts,bshd->bhtd', ap, vp)
        sv = sv + jnp.sum(lop, axis=-1)
        return (acc, sv), None

    (out, _), _ = lax.scan(page_step, (out, carry), page_table.T[::-1])
    return out.transpose(0, 2, 1, 3).reshape(bsz, nq, h * d)


def conv_ffn(x, buf, w_up, w_conv, b_conv, w_down):
    s = x.shape[1]
    ug = x @ w_up
    u, g = ug[..., :D_FF], ug[..., D_FF:]
    pad = jnp.concatenate([buf.astype(u.dtype), u], axis=1)
    uc = b_conv
    for j in range(CONV_W):
        uc = uc + w_conv[j] * pad[:, j:j + s]
    hdn = jax.nn.gelu(uc) * g
    return hdn @ w_down, pad[:, -(CONV_W - 1):]


def setup_inputs(seed: int = 0) -> dict:
    key = jax.random.key(seed)
    k = jax.random.split(key, 40)
    f32 = jnp.float32
    n_pages = PAST_LEN // PAGE_SIZE
    n_phys = (DEC_BATCH * n_pages * 5) // 4

    def nrm(i, shape, scale):
        return jax.random.normal(k[i], shape, f32) * scale

    n_idx = jnp.arange(S5_STATE, dtype=f32)
    page_table = jax.random.permutation(k[8], n_phys)[:DEC_BATCH * n_pages].reshape(DEC_BATCH, n_pages).astype(jnp.int32)
    return {
        'x_prompt': nrm(0, (BATCH, SEQ, D_MODEL), 1.0),
        'x_sample': nrm(1, (DEC_BATCH, DEC_SEQ, D_MODEL), 1.0),
        'state_s5_re': nrm(2, (N_EVEN, DEC_BATCH, S5_GROUPS, S5_STATE), 0.5),
        'state_s5_im': nrm(3, (N_EVEN, DEC_BATCH, S5_GROUPS, S5_STATE), 0.5),
        'state_gla': nrm(4, (N_EVEN, DEC_BATCH, GLA_HEADS, GLA_DK, GLA_DV), 1.0),
        'cache_k': nrm(5, (N_ODD, n_phys, PAGE_SIZE, SB_HEADS, SB_DIM), 1.0),
        'cache_v': nrm(6, (N_ODD, n_phys, PAGE_SIZE, SB_HEADS, SB_DIM), 1.0),
        'state_ffn_conv': nrm(7, (DEPTH, DEC_BATCH, CONV_W - 1, D_FF), 1.0),
        'page_table': page_table,
        'norm_mix': 1.0 + nrm(9, (DEPTH, D_MODEL), 0.02),
        'norm_ffn': 1.0 + nrm(10, (DEPTH, D_MODEL), 0.02),
        'norm_final': 1.0 + nrm(11, (D_MODEL,), 0.02),
        'w_in0': nrm(12, (N_EVEN, D_MODEL, IN0_WIDTH), D_MODEL ** -0.5),
        's5_a_re': -0.5 + nrm(13, (N_EVEN, S5_GROUPS, S5_STATE), 0.01),
        's5_a_im': math.pi * n_idx + nrm(14, (N_EVEN, S5_GROUPS, S5_STATE), 0.01),
        's5_log_dt': jax.random.uniform(k[15], (N_EVEN, S5_GROUPS), f32, math.log(0.001), math.log(0.1)),
        's5_b_re': nrm(16, (N_EVEN, S5_GROUPS, S5_STATE, S5_GROUP), (2.0 * S5_GROUP) ** -0.5),
        's5_b_im': nrm(17, (N_EVEN, S5_GROUPS, S5_STATE, S5_GROUP), (2.0 * S5_GROUP) ** -0.5),
        's5_c_re': nrm(18, (N_EVEN, S5_GROUPS, S5_GROUP, S5_STATE), (2.0 * S5_STATE) ** -0.5),
        's5_c_im': nrm(19, (N_EVEN, S5_GROUPS, S5_GROUP, S5_STATE), (2.0 * S5_STATE) ** -0.5),
        's5_d': nrm(20, (N_EVEN, S5_WIDTH), 1.0),
        's5_w_glu': nrm(21, (N_EVEN, S5_WIDTH, S5_WIDTH), S5_WIDTH ** -0.5),
        's5_b_glu': nrm(22, (N_EVEN, S5_WIDTH), 0.01),
        'gla_w_gk': nrm(23, (N_EVEN, GLA_RANK, GLA_HEADS * GLA_DK), GLA_RANK ** -0.5),
        'gla_b_gk': nrm(24, (N_EVEN, GLA_HEADS * GLA_DK), 0.1),
        'gla_norm': 1.0 + nrm(25, (N_EVEN, GLA_DV), 0.02),
        'w_out0': nrm(26, (N_EVEN, D_MODEL, D_MODEL), D_MODEL ** -0.5),
        'w_qkv1': nrm(27, (N_ODD, D_MODEL, 3 * D_MODEL), D_MODEL ** -0.5),
        'w_out1': nrm(28, (N_ODD, D_MODEL, D_MODEL), D_MODEL ** -0.5),
        'sb_bias': SB_BIAS_INIT + nrm(33, (N_ODD, SB_HEADS), 0.5),
        'ffn_w_up': nrm(29, (DEPTH, D_MODEL, 2 * D_FF), D_MODEL ** -0.5),
        'ffn_w_conv': nrm(30, (DEPTH, CONV_W, D_FF), CONV_W ** -0.5),
        'ffn_b_conv': nrm(31, (DEPTH, D_FF), 0.01),
        'ffn_w_down': nrm(32, (DEPTH, D_FF, D_MODEL), D_FF ** -0.5),
    }


def reference(x_prompt, x_sample, state_s5_re, state_s5_im, state_gla, cache_k, cache_v, state_ffn_conv,
              page_table, norm_mix, norm_ffn, norm_final, w_in0, s5_a_re, s5_a_im, s5_log_dt, s5_b_re, s5_b_im,
              s5_c_re, s5_c_im, s5_d, s5_w_glu, s5_b_glu, gla_w_gk, gla_b_gk, gla_norm, w_out0, w_qkv1, w_out1,
              sb_bias, ffn_w_up, ffn_w_conv, ffn_b_conv, ffn_w_down):
    xp, xs = x_prompt, x_sample
    bp, bs = xp.shape[0], xs.shape[0]
    f32 = jnp.float32
    s5r_p, s5i_p, gla_p, k_p, v_p, conv_p = [], [], [], [], [], []
    s5r_s, s5i_s, gla_s, k_s, v_s, conv_s = [], [], [], [], [], []
    for l in range(DEPTH):
        i = l // 2
        hp = rmsnorm(xp, norm_mix[l])
        hs = rmsnorm(xs, norm_mix[l])
        if l % 2 == 0:
            ew = (w_in0[i], s5_a_re[i], s5_a_im[i], s5_log_dt[i], s5_b_re[i], s5_b_im[i], s5_c_re[i], s5_c_im[i],
                  s5_d[i], s5_w_glu[i], s5_b_glu[i], gla_w_gk[i], gla_b_gk[i], gla_norm[i], w_out0[i])
            z_s5 = jnp.zeros((bp, S5_GROUPS, S5_STATE), f32)
            z_gla = jnp.zeros((bp, GLA_HEADS, GLA_DK, GLA_DV), f32)
            yp, hr, hi, sg = even_mixer(hp, z_s5, z_s5, z_gla, *ew)
            ys, hr2, hi2, sg2 = even_mixer(hs, state_s5_re[i], state_s5_im[i], state_gla[i], *ew)
            s5r_p.append(hr)
            s5i_p.append(hi)
            gla_p.append(sg)
            s5r_s.append(hr2)
            s5i_s.append(hi2)
            gla_s.append(sg2)
        else:
            qp, kp, vp = sb_qkv(hp, w_qkv1[i])
            yp = sb_prompt(qp, kp, vp, sb_bias[i]).astype(hp.dtype) @ w_out1[i]
            qs, ks, vs = sb_qkv(hs, w_qkv1[i])
            ys = sb_sample(qs, ks, vs, cache_k, cache_v, i, page_table, sb_bias[i]).astype(hs.dtype) @ w_out1[i]
            k_p.append(kp)
            v_p.append(vp)
            k_s.append(ks)
            v_s.append(vs)
        xp = xp + yp
        xs = xs + ys
        hp = rmsnorm(xp, norm_ffn[l])
        hs = rmsnorm(xs, norm_ffn[l])
        fp, cp = conv_ffn(hp, jnp.zeros((bp, CONV_W - 1, D_FF), hp.dtype), ffn_w_up[l], ffn_w_conv[l], ffn_b_conv[l], ffn_w_down[l])
        fs, cs = conv_ffn(hs, state_ffn_conv[l], ffn_w_up[l], ffn_w_conv[l], ffn_b_conv[l], ffn_w_down[l])
        conv_p.append(cp)
        conv_s.append(cs)
        xp = xp + fp
        xs = xs + fs
    y_prompt = rmsnorm(xp, norm_final)
    y_sample = rmsnorm(xs, norm_final)
    return (y_prompt, y_sample,
            jnp.stack(s5r_p), jnp.stack(s5i_p), jnp.stack(gla_p), jnp.stack(k_p), jnp.stack(v_p), jnp.stack(conv_p),
            jnp.stack(s5r_s), jnp.stack(s5i_s), jnp.stack(gla_s), jnp.stack(k_s), jnp.stack(v_s), jnp.stack(conv_s))
```

```python
import functools
import math

import jax
import jax.numpy as jnp
import numpy as np
from jax import lax
from jax.experimental import pallas as pl
from jax.experimental.pallas import tpu as pltpu

F32 = jnp.float32
BF16 = jnp.bfloat16
EPS = 1e-6

S5_GROUP = 16
S5_CHUNK = 16
GLA_HEADS = 4
GLA_DK = 64
GLA_DV = 128
GLA_RANK = 16
GLA_GATE_NORM = 16.0
SB_HEADS = 16
SB_DIM = 64
CONV_W = 3
LANES = 128
VMEM_LIMIT = 56 * 1024 * 1024


def _cparams(sem):
    return pltpu.CompilerParams(dimension_semantics=sem, vmem_limit_bytes=VMEM_LIMIT)


def _tile(n, pref):
    if n <= pref:
        return n
    t = pref
    while n % t:
        t //= 2
    return t


def _rmsnorm(x, g):
    y = x * lax.rsqrt(jnp.mean(x * x, axis=-1, keepdims=True) + EPS)
    return y * g


def _split2(x):
    hi = x.astype(BF16)
    lo = (x - hi.astype(F32)).astype(BF16)
    return hi, lo


def _split3(x):
    hi = x.astype(BF16)
    r = x - hi.astype(F32)
    mid = r.astype(BF16)
    lo = (r - mid.astype(F32)).astype(BF16)
    return hi, mid, lo


def _dot(a, b):
    return jnp.dot(a, b, preferred_element_type=F32)


def _dot_nt(a, b):
    return lax.dot_general(a, b, (((1,), (1,)), ((), ())), preferred_element_type=F32)


def _dot_tn(a, b):
    return lax.dot_general(a, b, (((0,), (0,)), ((), ())), preferred_element_type=F32)


def _log_sigmoid_pair(z):
    lsp = jnp.minimum(z, 0.0) - jnp.log1p(jnp.exp(-jnp.abs(z)))
    return lsp, lsp - z


def _inproj_kernel(x_ref, g_ref, w_ref, o_ref):
    h = _rmsnorm(x_ref[...], g_ref[...]).astype(BF16)
    o_ref[...] = _dot(h, w_ref[...])


def _inproj(x, g, w):
    t, d = x.shape
    n = w.shape[1]
    tm = _tile(t, 512)
    return pl.pallas_call(
        _inproj_kernel,
        grid=(t // tm,),
        in_specs=[pl.BlockSpec((tm, d), lambda i: (i, 0)),
                  pl.BlockSpec((1, d), lambda i: (0, 0)),
                  pl.BlockSpec((d, n), lambda i: (0, 0))],
        out_specs=pl.BlockSpec((tm, n), lambda i: (i, 0)),
        out_shape=jax.ShapeDtypeStruct((t, n), F32),
        compiler_params=_cparams(("parallel",)),
        name="inproj",
    )(x, g, w)


def _qkv_kernel(x_ref, g_ref, w_ref, qb_ref, k_ref, v_ref, kb_ref, vb_ref):
    d = x_ref.shape[1]
    h = _rmsnorm(x_ref[...], g_ref[...]).astype(BF16)
    p = _dot(h, w_ref[...])
    q, k, v = p[:, :d], p[:, d:2 * d], p[:, 2 * d:]
    qb_ref[...] = (q * (SB_DIM ** -0.5)).astype(BF16)
    k_ref[...] = k
    v_ref[...] = v
    kb_ref[...] = k.astype(BF16)
    vb_ref[...] = v.astype(BF16)


def _qkv(x, g, w):
    t, d = x.shape
    tm = _tile(t, 256)
    row = pl.BlockSpec((tm, d), lambda i: (i, 0))
    return pl.pallas_call(
        _qkv_kernel,
        grid=(t // tm,),
        in_specs=[row, pl.BlockSpec((1, d), lambda i: (0, 0)), pl.BlockSpec((d, 3 * d), lambda i: (0, 0))],
        out_specs=[row, row, row, row, row],
        out_shape=[jax.ShapeDtypeStruct((t, d), BF16), jax.ShapeDtypeStruct((t, d), F32),
                   jax.ShapeDtypeStruct((t, d), F32), jax.ShapeDtypeStruct((t, d), BF16),
                   jax.ShapeDtypeStruct((t, d), BF16)],
        compiler_params=_cparams(("parallel",)),
        name="qkv",
    )(x, g, w)


def _proj_res_kernel(x_ref, a_ref, w_ref, o_ref):
    o_ref[...] = x_ref[...] + _dot(a_ref[...].astype(BF16), w_ref[...])


def _proj_res(x, a, w):
    t, d = x.shape
    k = a.shape[1]
    tm = _tile(t, 512)
    return pl.pallas_call(
        _proj_res_kernel,
        grid=(t // tm,),
        in_specs=[pl.BlockSpec((tm, d), lambda i: (i, 0)),
                  pl.BlockSpec((tm, k), lambda i: (i, 0)),
                  pl.BlockSpec((k, d), lambda i: (0, 0))],
        out_specs=pl.BlockSpec((tm, d), lambda i: (i, 0)),
        out_shape=jax.ShapeDtypeStruct((t, d), F32),
        compiler_params=_cparams(("parallel",)),
        name="proj_res",
    )(x, a, w)


def _s5_operators(a_re, a_im, log_dt, b_re, b_im, c_re, c_im, chunk, n_chunks):
    hp = lax.Precision.HIGHEST
    g, p = a_re.shape
    dt = jnp.exp(log_dt)[:, None]
    mag = jnp.exp(a_re * dt)
    ab_re, ab_im = mag * jnp.cos(a_im * dt), mag * jnp.sin(a_im * dt)
    den = a_re * a_re + a_im * a_im
    nr, ni = ab_re - 1.0, ab_im
    coef_re = (nr * a_re + ni * a_im) / den
    coef_im = (ni * a_re - nr * a_im) / den
    bb_re = coef_re[..., None] * b_re - coef_im[..., None] * b_im
    bb_im = coef_re[..., None] * b_im + coef_im[..., None] * b_re
    pr, pi = [jnp.ones_like(ab_re)], [jnp.zeros_like(ab_re)]
    for _ in range(chunk):
        r0, i0 = pr[-1], pi[-1]
        pr.append(r0 * ab_re - i0 * ab_im)
        pi.append(r0 * ab_im + i0 * ab_re)
    pw_re, pw_im = jnp.stack(pr, 1), jnp.stack(pi, 1)
    ca_re = c_re[:, None] * pw_re[:, :, None, :] - c_im[:, None] * pw_im[:, :, None, :]
    ca_im = c_re[:, None] * pw_im[:, :, None, :] + c_im[:, None] * pw_re[:, :, None, :]
    taps = (jnp.einsum('gtcp,gpd->gtcd', ca_re[:, :chunk], bb_re, precision=hp)
            - jnp.einsum('gtcp,gpd->gtcd', ca_im[:, :chunk], bb_im, precision=hp))
    s_idx = np.arange(chunk)[:, None]
    i_idx = np.arange(chunk)[None, :]
    tau = np.clip(i_idx - s_idx, 0, chunk - 1)
    m = taps[:, tau]
    m = jnp.where((i_idx >= s_idx)[None, :, :, None, None], m, 0.0)
    m = m.transpose(0, 1, 4, 2, 3).reshape(g, chunk * S5_GROUP, chunk * S5_GROUP)
    rev_re, rev_im = pw_re[:, chunk - 1::-1][:, :chunk], pw_im[:, chunk - 1::-1][:, :chunk]
    w_re = (rev_re[:, :, None, :] * bb_re.transpose(0, 2, 1)[:, None]
            - rev_im[:, :, None, :] * bb_im.transpose(0, 2, 1)[:, None]).reshape(g, chunk * S5_GROUP, p)
    w_im = (rev_re[:, :, None, :] * bb_im.transpose(0, 2, 1)[:, None]
            + rev_im[:, :, None, :] * bb_re.transpose(0, 2, 1)[:, None]).reshape(g, chunk * S5_GROUP, p)
    v_re = ca_re[:, 1:].transpose(0, 3, 1, 2).reshape(g, p, chunk * S5_GROUP)
    v_im = -ca_im[:, 1:].transpose(0, 3, 1, 2).reshape(g, p, chunk * S5_GROUP)
    sr, si = [pw_re[:, chunk]], [pw_im[:, chunk]]
    n_steps = max(1, int(math.log2(n_chunks))) if n_chunks > 1 else 1
    for _ in range(n_steps - 1):
        r0, i0 = sr[-1], si[-1]
        sr.append(r0 * r0 - i0 * i0)
        si.append(2.0 * r0 * i0)
    return (m.astype(BF16), w_re.astype(BF16), w_im.astype(BF16), v_re.astype(BF16), v_im.astype(BF16),
            jnp.stack(sr, 1), jnp.stack(si, 1))


def _s5_kernel(u_ref, m_ref, wre_ref, wim_ref, vre_ref, vim_ref, are_ref, aim_ref, h0re_ref, h0im_ref,
               y_ref, hre_ref, him_ref, *, n_chunks, has_h0):
    u = u_ref[0]
    u_hi, u_lo = _split2(u)
    y = _dot(u_hi, m_ref[0])
    s_re = _dot(u_hi, wre_ref[0]) + _dot(u_lo, wre_ref[0])
    s_im = _dot(u_hi, wim_ref[0]) + _dot(u_lo, wim_ref[0])
    pw_re, pw_im = are_ref[0], aim_ref[0]
    if has_h0:
        h0r, h0i = h0re_ref[0], h0im_ref[0]
        ar, ai = pw_re[0:1], pw_im[0:1]
        s_re = s_re + ar * h0r - ai * h0i
        s_im = s_im + ar * h0i + ai * h0r
        hs_re, hs_im = h0r, h0i
    else:
        row = lax.broadcasted_iota(jnp.int32, s_re.shape, 0) % n_chunks
        shift, j = 1, 0
        while shift < n_chunks:
            ar, ai = pw_re[j:j + 1], pw_im[j:j + 1]
            keep = row >= shift
            pr = jnp.where(keep, pltpu.roll(s_re, shift, 0), 0.0)
            pi = jnp.where(keep, pltpu.roll(s_im, shift, 0), 0.0)
            s_re, s_im = s_re + ar * pr - ai * pi, s_im + ar * pi + ai * pr
            shift, j = shift * 2, j + 1
        keep = row >= 1
        hs_re = jnp.where(keep, pltpu.roll(s_re, 1, 0), 0.0)
        hs_im = jnp.where(keep, pltpu.roll(s_im, 1, 0), 0.0)
    y = y + _dot(hs_re.astype(BF16), vre_ref[0]) + _dot(hs_im.astype(BF16), vim_ref[0])
    y_ref[0] = y
    hre_ref[0] = s_re
    him_ref[0] = s_im


def _s5(u_g, ops, h0_re, h0_im, n_chunks):
    g, r, lc = u_g.shape
    m, w_re, w_im, v_re, v_im, a_re, a_im = ops
    p = w_re.shape[2]
    has_h0 = h0_re is not None
    if not has_h0:
        h0_re = h0_im = jnp.zeros((g, 8, p), F32)
    ns = a_re.shape[1]
    grp = lambda *shape: pl.BlockSpec((1,) + shape, lambda i: (i, 0, 0))
    return pl.pallas_call(
        functools.partial(_s5_kernel, n_chunks=n_chunks, has_h0=has_h0),
        grid=(g,),
        in_specs=[grp(r, lc), grp(lc, lc), grp(lc, p), grp(lc, p), grp(p, lc), grp(p, lc), grp(ns, p), grp(ns, p),
                  grp(h0_re.shape[1], p), grp(h0_re.shape[1], p)],
        out_specs=[grp(r, lc), grp(r, p), grp(r, p)],
        out_shape=[jax.ShapeDtypeStruct((g, r, lc), F32), jax.ShapeDtypeStruct((g, r, p), F32),
                   jax.ShapeDtypeStruct((g, r, p), F32)],
        compiler_params=_cparams(("parallel",)),
        name="s5",
    )(u_g, m, w_re, w_im, v_re, v_im, a_re, a_im, h0_re, h0_im)


def _gla_constants(tc):
    nl = int(math.log2(tc))
    t = np.arange(tc)
    blocks = []
    for j in range(nl):
        m = 1 << j
        base = t & ~(2 * m - 1)
        ref = base + m - 1
        col = np.arange(tc)[None, :]
        is_q = ((t >> j) & 1) == 1
        dq = (col > ref[:, None]) & (col <= t[:, None])
        dk = (col > t[:, None]) & (col <= ref[:, None])
        blocks.append(np.where(is_q[:, None], dq, dk))
    blocks.append(np.tril(np.ones((tc, tc), bool)))
    blocks.append(np.triu(np.ones((tc, tc), bool), 1))
    dall = np.concatenate(blocks, 0).astype(np.float32)
    x = t[:, None] ^ t[None, :]
    lvl = np.where(x > 0, np.floor(np.log2(np.maximum(x, 1))).astype(np.int32), -1)
    lvl = np.where(t[:, None] > t[None, :], lvl, np.where(t[:, None] == t[None, :], -1, -2)).astype(np.int32)
    return jnp.asarray(dall, BF16), jnp.asarray(lvl)


def _gla_kernel(q_ref, k_ref, v_ref, g_ref, r_ref, wgk_ref, bgk_ref, gn_ref, s0_ref, dall_ref, lvl_ref,
                o_ref, sfin_ref, st_ref, *, tc, valid):
    c = pl.program_id(1)
    nl = int(math.log2(tc))
    hdk = GLA_HEADS * GLA_DK

    @pl.when(c == 0)
    def _():
        st_ref[...] = s0_ref[0]

    q = q_ref[0] * (GLA_DK ** -0.5)
    k = k_ref[0]
    v = v_ref[0]
    x = _dot(r_ref[0].astype(BF16), wgk_ref[...]) + bgk_ref[...]
    gk = _log_sigmoid_pair(x)[0] / GLA_GATE_NORM
    if valid < tc:
        live = lax.broadcasted_iota(jnp.int32, gk.shape, 0) < valid
        gk = jnp.where(live, gk, 0.0)
        k = jnp.where(live, k, 0.0)
    g3 = jnp.concatenate(_split3(gk), axis=1)
    r3 = _dot(dall_ref[...], g3)
    ex = jnp.exp(r3[:, :hdk] + r3[:, hdk:2 * hdk] + r3[:, 2 * hdk:])
    e_cum = ex[nl * tc:(nl + 1) * tc]
    e_rem = ex[(nl + 1) * tc:]
    bl = _dot_tn(g3, jnp.ones((tc, GLA_DV), BF16))
    decay = jnp.exp(bl[:hdk] + bl[hdk:2 * hdk] + bl[2 * hdk:])

    row = lax.broadcasted_iota(jnp.int32, q.shape, 0)
    lane = lax.broadcasted_iota(jnp.int32, (tc, 2 * GLA_DK), 1)
    lvl = lvl_ref[...]
    q_in = (q * e_cum).astype(BF16)
    k_out = (k * e_rem).astype(BF16)
    xs = []
    for j in range(nl):
        is_q = ((row >> j) & 1) == 1
        xs.append((jnp.where(is_q, q, k) * ex[j * tc:(j + 1) * tc]).astype(BF16))
    qb, kb = q.astype(BF16), k.astype(BF16)
    gn = gn_ref[...]
    for h in range(GLA_HEADS):
        pair, sub = h // 2, h % 2
        ps = slice(pair * 2 * GLA_DK, (pair + 1) * 2 * GLA_DK)
        in_head = (lane < GLA_DK) if sub == 0 else (lane >= GLA_DK)
        att = jnp.where(lvl == -1, _dot_nt(jnp.where(in_head, qb[:, ps], 0), kb[:, ps]), 0.0)
        for j in range(nl):
            xp = xs[j][:, ps]
            att = jnp.where(lvl == j, _dot_nt(jnp.where(in_head, xp, 0), xp), att)
        vh = v[:, h * GLA_DV:(h + 1) * GLA_DV].astype(BF16)
        st = st_ref[pair]
        o = _dot(att.astype(BF16), vh) + _dot(jnp.where(in_head, q_in[:, ps], 0), st.astype(BF16))
        upd = _dot_tn(k_out[:, ps], vh)
        rsel = lax.broadcasted_iota(jnp.int32, upd.shape, 0)
        mine = (rsel < GLA_DK) if sub == 0 else (rsel >= GLA_DK)
        st_ref[pair] = jnp.where(mine, decay[ps] * st + upd, st)
        o = o * lax.rsqrt(jnp.mean(o * o, axis=-1, keepdims=True) + EPS) * gn
        gh = g_ref[0, :, h * GLA_DV:(h + 1) * GLA_DV]
        o_ref[0, :, h * GLA_DV:(h + 1) * GLA_DV] = o * (gh * jax.nn.sigmoid(gh))

    sfin_ref[0] = st_ref[...]


def _gla(p3, w_gk, b_gk, gla_norm, s0, tc, valid):
    b, s, _ = p3.shape
    hdk, hdv = GLA_HEADS * GLA_DK, GLA_HEADS * GLA_DV
    u_w = 2 * hdk
    dall, lvl = _gla_constants(tc)
    const = lambda shape: pl.BlockSpec(shape, lambda i, j: tuple(0 for _ in shape))
    return pl.pallas_call(
        functools.partial(_gla_kernel, tc=tc, valid=valid),
        grid=(b, s // tc),
        in_specs=[pl.BlockSpec((1, tc, hdk), lambda i, j: (i, j, u_w // hdk)),
                  pl.BlockSpec((1, tc, hdk), lambda i, j: (i, j, u_w // hdk + 1)),
                  pl.BlockSpec((1, tc, hdv), lambda i, j: (i, j, (u_w + 2 * hdk) // hdv)),
                  pl.BlockSpec((1, tc, hdv), lambda i, j: (i, j, (u_w + 2 * hdk) // hdv + 1)),
                  pl.BlockSpec((1, tc, LANES), lambda i, j: (i, j, (u_w + 2 * hdk + 2 * hdv) // LANES)),
                  const((LANES, hdk)), const((1, hdk)), const((1, GLA_DV)),
                  pl.BlockSpec((1, GLA_HEADS // 2, 2 * GLA_DK, GLA_DV), lambda i, j: (i, 0, 0, 0)),
                  const(dall.shape), const(lvl.shape)],
        out_specs=[pl.BlockSpec((1, tc, hdv), lambda i, j: (i, j, 0)),
                   pl.BlockSpec((1, GLA_HEADS // 2, 2 * GLA_DK, GLA_DV), lambda i, j: (i, 0, 0, 0))],
        out_shape=[jax.ShapeDtypeStruct((b, s, hdv), F32),
                   jax.ShapeDtypeStruct((b, GLA_HEADS // 2, 2 * GLA_DK, GLA_DV), F32)],
        scratch_shapes=[pltpu.VMEM((GLA_HEADS // 2, 2 * GLA_DK, GLA_DV), F32)],
        compiler_params=_cparams(("parallel", "arbitrary")),
        name="gla",
    )(p3, p3, p3, p3, p3, w_gk, b_gk, gla_norm, s0, dall, lvl)


def _mix_out_kernel(x_ref, yc_ref, u_ref, og_ref, d_ref, wglu_ref, bglu_ref, wtop_ref, wbot_ref, o_ref):
    y = yc_ref[...] + d_ref[...] * u_ref[...]
    z = jax.nn.gelu(y)
    s5 = z * jax.nn.sigmoid(_dot(z.astype(BF16), wglu_ref[...]) + bglu_ref[...])
    o_ref[...] = (x_ref[...] + _dot(s5.astype(BF16), wtop_ref[...])
                  + _dot(og_ref[...].astype(BF16), wbot_ref[...]))


def _mix_out(x, y_conv, p, o_gla, d_skip, w_glu, b_glu, w_top, w_bot):
    t, d = x.shape
    w5 = y_conv.shape[1]
    tm = _tile(t, 512)
    row = lambda w: pl.BlockSpec((tm, w), lambda i: (i, 0))
    const = lambda a, b: pl.BlockSpec((a, b), lambda i: (0, 0))
    return pl.pallas_call(
        _mix_out_kernel,
        grid=(t // tm,),
        in_specs=[row(d), row(w5), row(w5), row(o_gla.shape[1]), const(1, w5), const(w5, w5), const(1, w5),
                  const(w5, d), const(o_gla.shape[1], d)],
        out_specs=row(d),
        out_shape=jax.ShapeDtypeStruct((t, d), F32),
        compiler_params=_cparams(("parallel",)),
        name="mix_out",
    )(x, y_conv, p, o_gla, d_skip, w_glu, b_glu, w_top, w_bot)


def _ffn_kernel(x_ref, gn_ref, hist_ref, wu_ref, wg_ref, wc_ref, bc_ref, wd_ref, gf_ref,
                o_ref, hist_out_ref, h_scr, acc_scr, pad_scr, carry_scr, *, rs, hist_rows, final_norm):
    t = pl.program_id(1)
    c = pl.program_id(2)
    tm = x_ref.shape[1]

    @pl.when(c == 0)
    def _():
        h_scr[...] = _rmsnorm(x_ref[0], gn_ref[...]).astype(BF16)
        acc_scr[...] = jnp.zeros_like(acc_scr)

    @pl.when(t == 0)
    def _():
        carry_scr[c] = hist_ref[0]

    h = h_scr[...]
    u = _dot(h, wu_ref[...])
    g = _dot(h, wg_ref[...])
    pad_scr[0:hist_rows] = carry_scr[c]
    pad_scr[hist_rows:] = u
    wc = wc_ref[...]
    uc = bc_ref[...]
    for j in range(CONV_W):
        off = hist_rows - (CONV_W - 1 - j) * rs
        uc = uc + wc[j:j + 1] * pad_scr[off:off + tm]
    new_hist = pad_scr[tm:tm + hist_rows]
    carry_scr[c] = new_hist
    hist_out_ref[0] = new_hist
    hdn = jax.nn.gelu(uc) * g
    acc_scr[...] += _dot(hdn.astype(BF16), wd_ref[...])

    @pl.when(c == pl.num_programs(2) - 1)
    def _():
        y = x_ref[0] + acc_scr[...]
        if final_norm:
            y = _rmsnorm(y, gf_ref[...])
        o_ref[0] = y


def _ffn(x3, g_norm, hist, w_up, w_conv, b_conv, w_down, g_final, rs, final_norm):
    nb, s, d = x3.shape
    dff = w_down.shape[0]
    hist_rows = hist.shape[1]
    tm = _tile(s, 512)
    ck = dff // 2 if (dff // 2) % LANES == 0 else dff
    nck = dff // ck
    return pl.pallas_call(
        functools.partial(_ffn_kernel, rs=rs, hist_rows=hist_rows, final_norm=final_norm),
        grid=(nb, s // tm, nck),
        in_specs=[pl.BlockSpec((1, tm, d), lambda b, t, c: (b, t, 0)),
                  pl.BlockSpec((1, d), lambda b, t, c: (0, 0)),
                  pl.BlockSpec((1, hist_rows, ck), lambda b, t, c: (b, 0, c)),
                  pl.BlockSpec((d, ck), lambda b, t, c: (0, c)),
                  pl.BlockSpec((d, ck), lambda b, t, c: (0, nck + c)),
                  pl.BlockSpec((CONV_W, ck), lambda b, t, c: (0, c)),
                  pl.BlockSpec((1, ck), lambda b, t, c: (0, c)),
                  pl.BlockSpec((ck, d), lambda b, t, c: (c, 0)),
                  pl.BlockSpec((1, d), lambda b, t, c: (0, 0))],
        out_specs=[pl.BlockSpec((1, tm, d), lambda b, t, c: (b, t, 0)),
                   pl.BlockSpec((1, hist_rows, ck), lambda b, t, c: (b, 0, c))],
        out_shape=[jax.ShapeDtypeStruct((nb, s, d), F32), jax.ShapeDtypeStruct((nb, hist_rows, dff), F32)],
        scratch_shapes=[pltpu.VMEM((tm, d), BF16), pltpu.VMEM((tm, d), F32),
                        pltpu.VMEM((hist_rows + tm, ck), F32), pltpu.VMEM((nck, hist_rows, ck), F32)],
        compiler_params=_cparams(("parallel", "arbitrary", "arbitrary")),
        name="ffn",
    )(x3, g_norm, hist, w_up, w_up, w_conv, b_conv, w_down, g_final)


def _sb_tile(z, cb, mo, v, mask):
    tk = z.shape[1]
    lsp, lo = _log_sigmoid_pair(z)
    if mask is not None:
        lo = jnp.where(mask, lo, 0.0)
    lo_hi, lo_lo = _split2(lo)
    r = _dot(lo_hi, mo) + _dot(lo_lo, mo)
    a = jnp.exp(lsp + r[:, :tk] + cb)
    if mask is not None:
        a = jnp.where(mask, a, 0.0)
    return _dot(a.astype(BF16), v), cb + r[:, tk:]


def _sb_prompt_kernel(bias_ref, q_ref, k_ref, v_ref, mo_ref, o_ref, *, tq):
    hp = pl.program_id(1)
    qi = pl.program_id(2)
    q = q_ref[0]
    lane = lax.broadcasted_iota(jnp.int32, q.shape, 1)
    tri = lax.broadcasted_iota(jnp.int32, (tq, tq), 1) < lax.broadcasted_iota(jnp.int32, (tq, tq), 0)
    mo = mo_ref[...]
    outs = []
    for sub in range(2):
        in_head = (lane < SB_DIM) if sub == 0 else (lane >= SB_DIM)
        qh = jnp.where(in_head, q, 0)
        bias = bias_ref[2 * hp + sub]
        start = pl.multiple_of(qi * tq, tq)
        z = _dot_nt(qh, k_ref[0, pl.ds(start, tq), :]) + bias
        acc, cb = _sb_tile(z, jnp.zeros((tq, tq), F32), mo, v_ref[0, pl.ds(start, tq), :], tri)

        def body(i, carry):
            acc, cb = carry
            start = pl.multiple_of((qi - i) * tq, tq)
            z = _dot_nt(qh, k_ref[0, pl.ds(start, tq), :]) + bias
            pv, cb = _sb_tile(z, cb, mo, v_ref[0, pl.ds(start, tq), :], None)
            return acc + pv, cb

        acc, _ = lax.fori_loop(1, qi + 1, body, (acc, cb))
        outs.append(acc)
    o_ref[0] = jnp.where(lane < SB_DIM, outs[0], outs[1])


def _sb_mo(tk):
    j = np.arange(tk)[:, None]
    s = np.arange(tk)[None, :]
    return jnp.asarray(np.concatenate([(j > s), np.ones((tk, tk), bool)], 1).astype(np.float32), BF16)


def _sb_prompt(q_b, k_b, v_b, bias):
    b, s, hd = q_b.shape
    tq = _tile(s, 256)
    return pl.pallas_call(
        functools.partial(_sb_prompt_kernel, tq=tq),
        grid=(b, hd // LANES, s // tq),
        in_specs=[pl.BlockSpec(memory_space=pltpu.SMEM),
                  pl.BlockSpec((1, tq, LANES), lambda i, h, j: (i, j, h)),
                  pl.BlockSpec((1, s, LANES), lambda i, h, j: (i, 0, h)),
                  pl.BlockSpec((1, s, LANES), lambda i, h, j: (i, 0, h)),
                  pl.BlockSpec((tq, 2 * tq), lambda i, h, j: (0, 0))],
        out_specs=pl.BlockSpec((1, tq, LANES), lambda i, h, j: (i, j, h)),
        out_shape=jax.ShapeDtypeStruct((b, s, hd), F32),
        compiler_params=_cparams(("parallel", "parallel", "arbitrary")),
        name="sb_prompt",
    )(bias, q_b, k_b, v_b, _sb_mo(tq))


def _sb_sample_block(kb, vb, qbd, bias_l, mrev, carry, mask):
    z = _dot(kb, qbd) + bias_l
    lsp, lo = _log_sigmoid_pair(z)
    if mask is not None:
        lo = jnp.where(mask, lo, 0.0)
    lo_hi, lo_lo = _split2(lo)
    a = jnp.exp(lsp + _dot(mrev, lo_hi) + _dot(mrev, lo_lo) + carry)
    if mask is not None:
        a = jnp.where(mask, a, 0.0)
    pv = _dot_tn(a.astype(BF16), vb)
    return pv, carry + jnp.sum(lo, axis=0, keepdims=True)


def _sb_sample_kernel(pt_ref, qbd_ref, bias_ref, kn_ref, vn_ref, ck_ref, cv_ref, mnew_ref, mpage_ref,
                      o_ref, acc_ref, carry_ref, *, nq):
    j = pl.program_id(1)
    qbd = qbd_ref[0]
    bias_l = bias_ref[...]
    nh = SB_HEADS

    @pl.when(j == 0)
    def _():
        sn = kn_ref.shape[1]
        s_idx = lax.broadcasted_iota(jnp.int32, (sn, nq * nh), 0)
        q_idx = lax.broadcasted_iota(jnp.int32, (sn, nq * nh), 1) // nh
        pv, carry = _sb_sample_block(kn_ref[0].astype(BF16), vn_ref[0].astype(BF16), qbd, bias_l,
                                     mnew_ref[...], jnp.zeros((1, nq * nh), F32), s_idx < q_idx)
        acc_ref[...] = pv
        carry_ref[...] = carry

    pv, carry = _sb_sample_block(ck_ref[0, 0].astype(BF16), cv_ref[0, 0].astype(BF16), qbd, bias_l,
                                 mpage_ref[...], carry_ref[...], None)
    acc_ref[...] += pv
    carry_ref[...] = carry

    @pl.when(j == pl.num_programs(1) - 1)
    def _():
        acc = acc_ref[...]
        r_h = lax.broadcasted_iota(jnp.int32, acc.shape, 0) % nh
        l_h = lax.broadcasted_iota(jnp.int32, acc.shape, 1) // SB_DIM
        sel = jnp.where(r_h == l_h, acc, 0.0)
        o_ref[0] = jnp.sum(sel.reshape(nq, nh, acc.shape[1]), axis=1)


def _mrev(n):
    s = np.arange(n)[:, None]
    j = np.arange(n)[None, :]
    return jnp.asarray((j > s).astype(np.float32), BF16)


def _sb_sample(q_b, k_new, v_new, cache_k, cache_v, layer, page_table, bias):
    b, nq, hd = q_b.shape
    nh = SB_HEADS
    n_pages = page_table.shape[1]
    page = cache_k.shape[2]
    sn = 16
    qs = q_b.reshape(b, nq, nh, SB_DIM)
    eye = jnp.eye(nh, dtype=BF16)
    qbd = (qs.transpose(0, 2, 3, 1)[:, :, :, :, None] * eye[None, :, None, None, :]).reshape(b, hd, nq * nh)
    bias_l = jnp.tile(bias, nq)[None, :]
    pad = lambda a: jnp.pad(a, ((0, 0), (0, sn - nq), (0, 0)))
    return pl.pallas_call(
        functools.partial(_sb_sample_kernel, nq=nq),
        grid_spec=pltpu.PrefetchScalarGridSpec(
            num_scalar_prefetch=1,
            grid=(b, n_pages),
            in_specs=[pl.BlockSpec((1, hd, nq * nh), lambda i, j, pt: (i, 0, 0)),
                      pl.BlockSpec((1, nq * nh), lambda i, j, pt: (0, 0)),
                      pl.BlockSpec((1, sn, hd), lambda i, j, pt: (i, 0, 0)),
                      pl.BlockSpec((1, sn, hd), lambda i, j, pt: (i, 0, 0)),
                      pl.BlockSpec((1, 1, page, hd), lambda i, j, pt: (layer, pt[i, n_pages - 1 - j], 0, 0)),
                      pl.BlockSpec((1, 1, page, hd), lambda i, j, pt: (layer, pt[i, n_pages - 1 - j], 0, 0)),
                      pl.BlockSpec((sn, sn), lambda i, j, pt: (0, 0)),
                      pl.BlockSpec((page, page), lambda i, j, pt: (0, 0))],
            out_specs=pl.BlockSpec((1, nq, hd), lambda i, j, pt: (i, 0, 0)),
            scratch_shapes=[pltpu.VMEM((nq * nh, hd), F32), pltpu.VMEM((1, nq * nh), F32)],
        ),
        out_shape=jax.ShapeDtypeStruct((b, nq, hd), F32),
        compiler_params=_cparams(("parallel", "arbitrary")),
        name="sb_sample",
    )(page_table, qbd, bias_l, pad(k_new), pad(v_new), cache_k, cache_v, _mrev(sn), _mrev(page))


def _even_layer(x3, g_mix, w_in_p, s5_ops, d_skip, w_glu, b_glu, w_gk_p, b_gk, gla_norm, w_top, w_bot,
                s5_h0, gla_s0, chunk):
    b, s, d = x3.shape
    s5_w = d_skip.shape[1]
    ng = s5_w // S5_GROUP
    nc = s // chunk
    x2 = x3.reshape(b * s, d)
    p = _inproj(x2, g_mix, w_in_p)
    u_g = (p[:, :s5_w].reshape(b, nc, chunk, ng, S5_GROUP).transpose(3, 0, 1, 2, 4)
           .reshape(ng, b * nc, chunk * S5_GROUP))
    if s5_h0 is None:
        h0r = h0i = None
    else:
        h0r, h0i = s5_h0[0].transpose(1, 0, 2), s5_h0[1].transpose(1, 0, 2)
    y_g, hre, him = _s5(u_g, s5_ops, h0r, h0i, nc)
    y_conv = (y_g.reshape(ng, b, nc, chunk, S5_GROUP).transpose(1, 2, 3, 0, 4).reshape(b * s, s5_w))
    p_state = hre.shape[2]
    fin = lambda a: a.reshape(ng, b, nc, p_state)[:, :, -1].transpose(1, 0, 2)
    p3 = p.reshape(b, s, p.shape[1])
    gla_tc = _tile(s, 128) if s >= 16 else 16
    gla_valid = min(s, gla_tc)
    sp = p3 if s >= gla_tc else jnp.pad(p3, ((0, 0), (0, gla_tc - s), (0, 0)))
    s0 = (jnp.zeros((b, GLA_HEADS // 2, 2 * GLA_DK, GLA_DV), F32) if gla_s0 is None
          else gla_s0.reshape(b, GLA_HEADS // 2, 2 * GLA_DK, GLA_DV))
    o_gla, s_fin = _gla(sp, w_gk_p, b_gk, gla_norm, s0, gla_tc, gla_valid)
    o_gla = o_gla[:, :s].reshape(b * s, o_gla.shape[2])
    x2 = _mix_out(x2, y_conv, p, o_gla, d_skip, w_glu, b_glu, w_top, w_bot)
    return x2.reshape(b, s, d), fin(hre), fin(him), s_fin.reshape(b, GLA_HEADS, GLA_DK, GLA_DV)


def kernel(x_prompt, x_sample, state_s5_re, state_s5_im, state_gla, cache_k, cache_v, state_ffn_conv, page_table, norm_mix, norm_ffn, norm_final, w_in0, s5_a_re, s5_a_im, s5_log_dt, s5_b_re, s5_b_im, s5_c_re, s5_c_im, s5_d, s5_w_glu, s5_b_glu, gla_w_gk, gla_b_gk, gla_norm, w_out0, w_qkv1, w_out1, sb_bias, ffn_w_up, ffn_w_conv, ffn_b_conv, ffn_w_down):
    bp, sp, d = x_prompt.shape
    bs, ss, _ = x_sample.shape
    depth = norm_mix.shape[0]
    dff = ffn_w_down.shape[1]
    s5_w = s5_d.shape[1]
    xp, xs = x_prompt, x_sample
    outs_p = {k: [] for k in ("s5r", "s5i", "gla", "k", "v", "conv")}
    outs_s = {k: [] for k in ("s5r", "s5i", "gla", "k", "v", "conv")}
    hist_rows_p = 8
    for l in range(depth):
        i = l // 2
        g_mix = norm_mix[l][None, :]
        if l % 2 == 0:
            in0 = w_in0.shape[2]
            in0p = -(-in0 // LANES) * LANES
            w_in_p = jnp.pad(w_in0[i], ((0, 0), (0, in0p - in0))).astype(BF16)
            w_gk_p = jnp.pad(gla_w_gk[i], ((0, LANES - GLA_RANK), (0, 0))).astype(BF16)
            common = (s5_d[i][None, :], s5_w_glu[i].astype(BF16), s5_b_glu[i][None, :], w_gk_p,
                      gla_b_gk[i][None, :], gla_norm[i][None, :], w_out0[i][:s5_w].astype(BF16),
                      w_out0[i][s5_w:].astype(BF16))
            s5_args = (s5_a_re[i], s5_a_im[i], s5_log_dt[i], s5_b_re[i], s5_b_im[i], s5_c_re[i], s5_c_im[i])
            chunk_p = _tile(sp, S5_CHUNK)
            ops_p = _s5_operators(*s5_args, chunk_p, sp // chunk_p)
            ops_s = _s5_operators(*s5_args, ss, 1)
            xp, hr, hi, sg = _even_layer(xp, g_mix, w_in_p, ops_p, *common, None, None, chunk_p)
            xs, hr2, hi2, sg2 = _even_layer(xs, g_mix, w_in_p, ops_s, *common,
                                            (state_s5_re[i], state_s5_im[i]), state_gla[i], ss)
            for o, vals in ((outs_p, (hr, hi, sg)), (outs_s, (hr2, hi2, sg2))):
                o["s5r"].append(vals[0])
                o["s5i"].append(vals[1])
                o["gla"].append(vals[2])
        else:
            w_qkv = w_qkv1[i].astype(BF16)
            w_out = w_out1[i].astype(BF16)
            qb, kp, vp, kb, vb = _qkv(xp.reshape(bp * sp, d), g_mix, w_qkv)
            att = _sb_prompt(qb.reshape(bp, sp, d), kb.reshape(bp, sp, d), vb.reshape(bp, sp, d), sb_bias[i])
            xp = _proj_res(xp.reshape(bp * sp, d), att.reshape(bp * sp, d), w_out).reshape(bp, sp, d)
            qsb, ks, vs, _, _ = _qkv(xs.reshape(bs * ss, d), g_mix, w_qkv)
            att_s = _sb_sample(qsb.reshape(bs, ss, d), ks.reshape(bs, ss, d), vs.reshape(bs, ss, d),
                               cache_k.reshape(cache_k.shape[0], cache_k.shape[1], cache_k.shape[2], d),
                               cache_v.reshape(cache_v.shape[0], cache_v.shape[1], cache_v.shape[2], d),
                               i, page_table, sb_bias[i])
            xs = _proj_res(xs.reshape(bs * ss, d), att_s.reshape(bs * ss, d), w_out).reshape(bs, ss, d)
            outs_p["k"].append(kp.reshape(bp, sp, SB_HEADS, SB_DIM))
            outs_p["v"].append(vp.reshape(bp, sp, SB_HEADS, SB_DIM))
            outs_s["k"].append(ks.reshape(bs, ss, SB_HEADS, SB_DIM))
            outs_s["v"].append(vs.reshape(bs, ss, SB_HEADS, SB_DIM))
        final = l == depth - 1
        w_up = ffn_w_up[l].astype(BF16)
        w_down = ffn_w_down[l].astype(BF16)
        ffn_args = (w_up, ffn_w_conv[l], ffn_b_conv[l][None, :], w_down, norm_final[None, :])
        xp, hist_p = _ffn(xp, norm_ffn[l][None, :], jnp.zeros((bp, hist_rows_p, dff), F32), *ffn_args, 1, final)
        outs_p["conv"].append(hist_p[:, hist_rows_p - (CONV_W - 1):])
        hist_s = state_ffn_conv[l].transpose(1, 0, 2).reshape(1, (CONV_W - 1) * bs, dff)
        xs_t, hist_s = _ffn(xs.transpose(1, 0, 2).reshape(1, ss * bs, d), norm_ffn[l][None, :], hist_s,
                            *ffn_args, bs, final)
        xs = xs_t.reshape(ss, bs, d).transpose(1, 0, 2)
        outs_s["conv"].append(hist_s.reshape(CONV_W - 1, bs, dff).transpose(1, 0, 2))
    st = jnp.stack
    return (xp, xs,
            st(outs_p["s5r"]), st(outs_p["s5i"]), st(outs_p["gla"]), st(outs_p["k"]), st(outs_p["v"]), st(outs_p["conv"]),
            st(outs_s["s5r"]), st(outs_s["s5i"]), st(outs_s["gla"]), st(outs_s["k"]), st(outs_s["v"]), st(outs_s["conv"]))
```

```python
import functools
import math

import jax
import jax.numpy as jnp
import numpy as np
from jax import lax
from jax.experimental import pallas as pl
from jax.experimental.pallas import tpu as pltpu

F32 = jnp.float32
BF16 = jnp.bfloat16
EPS = 1e-6

S5_GROUP = 16
S5_CHUNK = 16
GLA_HEADS = 4
GLA_DK = 64
GLA_DV = 128
GLA_RANK = 16
GLA_GATE_NORM = 16.0
SB_HEADS = 16
SB_DIM = 64
CONV_W = 3
LOG2E = math.log2(math.e)
SB_QSCALE = SB_DIM ** -0.5 * LOG2E
LANES = 128
VMEM_LIMIT = 56 * 1024 * 1024


def _cparams(sem):
    return pltpu.CompilerParams(dimension_semantics=sem, vmem_limit_bytes=VMEM_LIMIT)


def _tile(n, pref):
    if n <= pref:
        return n
    t = pref
    while n % t:
        t //= 2
    return t


def _rmsnorm(x, g):
    y = x * lax.rsqrt(jnp.mean(x * x, axis=-1, keepdims=True) + EPS)
    return y * g


def _split2(x):
    hi = x.astype(BF16)
    lo = (x - hi.astype(F32)).astype(BF16)
    return hi, lo


def _split3(x):
    hi = x.astype(BF16)
    r = x - hi.astype(F32)
    mid = r.astype(BF16)
    lo = (r - mid.astype(F32)).astype(BF16)
    return hi, mid, lo


def _dot(a, b):
    return jnp.dot(a, b, preferred_element_type=F32)


def _dot_nt(a, b):
    return lax.dot_general(a, b, (((1,), (1,)), ((), ())), preferred_element_type=F32)


def _dot_tn(a, b):
    return lax.dot_general(a, b, (((0,), (0,)), ((), ())), preferred_element_type=F32)


def _log_sigmoid_pair(z):
    lsp = jnp.minimum(z, 0.0) - jnp.log1p(jnp.exp(-jnp.abs(z)))
    return lsp, lsp - z


def _inproj_kernel(x_ref, g_ref, w_ref, o_ref):
    h = _rmsnorm(x_ref[...], g_ref[...]).astype(BF16)
    o_ref[...] = _dot(h, w_ref[...])


def _inproj(x, g, w):
    t, d = x.shape
    n = w.shape[1]
    tm = _tile(t, 512)
    return pl.pallas_call(
        _inproj_kernel,
        grid=(t // tm,),
        in_specs=[pl.BlockSpec((tm, d), lambda i: (i, 0)),
                  pl.BlockSpec((1, d), lambda i: (0, 0)),
                  pl.BlockSpec((d, n), lambda i: (0, 0))],
        out_specs=pl.BlockSpec((tm, n), lambda i: (i, 0)),
        out_shape=jax.ShapeDtypeStruct((t, n), F32),
        compiler_params=_cparams(("parallel",)),
        name="inproj",
    )(x, g, w)


def _qkv_kernel(x_ref, g_ref, w_ref, qb_ref, k_ref, v_ref, kb_ref, vb_ref):
    d = x_ref.shape[1]
    h = _rmsnorm(x_ref[...], g_ref[...]).astype(BF16)
    p = _dot(h, w_ref[...])
    q, k, v = p[:, :d], p[:, d:2 * d], p[:, 2 * d:]
    qb_ref[...] = (q * SB_QSCALE).astype(BF16)
    k_ref[...] = k
    v_ref[...] = v
    kb_ref[...] = k.astype(BF16)
    vb_ref[...] = v.astype(BF16)


def _qkv(x, g, w):
    t, d = x.shape
    tm = _tile(t, 256)
    row = pl.BlockSpec((tm, d), lambda i: (i, 0))
    return pl.pallas_call(
        _qkv_kernel,
        grid=(t // tm,),
        in_specs=[row, pl.BlockSpec((1, d), lambda i: (0, 0)), pl.BlockSpec((d, 3 * d), lambda i: (0, 0))],
        out_specs=[row, row, row, row, row],
        out_shape=[jax.ShapeDtypeStruct((t, d), BF16), jax.ShapeDtypeStruct((t, d), F32),
                   jax.ShapeDtypeStruct((t, d), F32), jax.ShapeDtypeStruct((t, d), BF16),
                   jax.ShapeDtypeStruct((t, d), BF16)],
        compiler_params=_cparams(("parallel",)),
        name="qkv",
    )(x, g, w)


def _qkv_t_kernel(x_ref, g_ref, wq_ref, wkvt_ref, qb_ref, kt_ref, vt_ref, ktb_ref, vtb_ref):
    d = x_ref.shape[2]
    h = _rmsnorm(x_ref[0], g_ref[...]).astype(BF16)
    qb_ref[0] = (_dot(h, wq_ref[...]) * SB_QSCALE).astype(BF16)
    kvt = _dot_nt(wkvt_ref[...], h)
    kt_ref[0] = kvt[:d]
    vt_ref[0] = kvt[d:]
    ktb_ref[0, 0] = kvt[:d].astype(BF16)
    vtb_ref[0, 0] = kvt[d:].astype(BF16)


def _qkv_t(x3, g, wq, wkvt, tm):
    b, s, d = x3.shape
    nt = s // tm
    return pl.pallas_call(
        _qkv_t_kernel,
        grid=(b, nt),
        in_specs=[pl.BlockSpec((1, tm, d), lambda i, t: (i, t, 0)),
                  pl.BlockSpec((1, d), lambda i, t: (0, 0)),
                  pl.BlockSpec((d, d), lambda i, t: (0, 0)),
                  pl.BlockSpec((2 * d, d), lambda i, t: (0, 0))],
        out_specs=[pl.BlockSpec((1, tm, d), lambda i, t: (i, t, 0)),
                   pl.BlockSpec((1, d, tm), lambda i, t: (i, 0, t)),
                   pl.BlockSpec((1, d, tm), lambda i, t: (i, 0, t)),
                   pl.BlockSpec((1, 1, d, tm), lambda i, t: (i, t, 0, 0)),
                   pl.BlockSpec((1, 1, d, tm), lambda i, t: (i, t, 0, 0))],
        out_shape=[jax.ShapeDtypeStruct((b, s, d), BF16), jax.ShapeDtypeStruct((b, d, s), F32),
                   jax.ShapeDtypeStruct((b, d, s), F32), jax.ShapeDtypeStruct((b, nt, d, tm), BF16),
                   jax.ShapeDtypeStruct((b, nt, d, tm), BF16)],
        compiler_params=_cparams(("parallel", "parallel")),
        name="qkv_t",
    )(x3, g, wq, wkvt)


def _proj_res_kernel(x_ref, a_ref, w_ref, o_ref):
    o_ref[...] = x_ref[...] + _dot(a_ref[...].astype(BF16), w_ref[...])


def _proj_res(x, a, w):
    t, d = x.shape
    k = a.shape[1]
    tm = _tile(t, 512)
    return pl.pallas_call(
        _proj_res_kernel,
        grid=(t // tm,),
        in_specs=[pl.BlockSpec((tm, d), lambda i: (i, 0)),
                  pl.BlockSpec((tm, k), lambda i: (i, 0)),
                  pl.BlockSpec((k, d), lambda i: (0, 0))],
        out_specs=pl.BlockSpec((tm, d), lambda i: (i, 0)),
        out_shape=jax.ShapeDtypeStruct((t, d), F32),
        compiler_params=_cparams(("parallel",)),
        name="proj_res",
    )(x, a, w)


def _s5_operators(a_re, a_im, log_dt, b_re, b_im, c_re, c_im, chunk, n_chunks):
    hp = lax.Precision.HIGHEST
    g, p = a_re.shape
    dt = jnp.exp(log_dt)[:, None]
    mag = jnp.exp(a_re * dt)
    ab_re, ab_im = mag * jnp.cos(a_im * dt), mag * jnp.sin(a_im * dt)
    den = a_re * a_re + a_im * a_im
    nr, ni = ab_re - 1.0, ab_im
    coef_re = (nr * a_re + ni * a_im) / den
    coef_im = (ni * a_re - nr * a_im) / den
    bb_re = coef_re[..., None] * b_re - coef_im[..., None] * b_im
    bb_im = coef_re[..., None] * b_im + coef_im[..., None] * b_re
    pr, pi = [jnp.ones_like(ab_re)], [jnp.zeros_like(ab_re)]
    for _ in range(chunk):
        r0, i0 = pr[-1], pi[-1]
        pr.append(r0 * ab_re - i0 * ab_im)
        pi.append(r0 * ab_im + i0 * ab_re)
    pw_re, pw_im = jnp.stack(pr, 1), jnp.stack(pi, 1)
    ca_re = c_re[:, None] * pw_re[:, :, None, :] - c_im[:, None] * pw_im[:, :, None, :]
    ca_im = c_re[:, None] * pw_im[:, :, None, :] + c_im[:, None] * pw_re[:, :, None, :]
    taps = (jnp.einsum('gtcp,gpd->gtcd', ca_re[:, :chunk], bb_re, precision=hp)
            - jnp.einsum('gtcp,gpd->gtcd', ca_im[:, :chunk], bb_im, precision=hp))
    s_idx = np.arange(chunk)[:, None]
    i_idx = np.arange(chunk)[None, :]
    tau = np.clip(i_idx - s_idx, 0, chunk - 1)
    m = taps[:, tau]
    m = jnp.where((i_idx >= s_idx)[None, :, :, None, None], m, 0.0)
    m = m.transpose(0, 1, 4, 2, 3).reshape(g, chunk * S5_GROUP, chunk * S5_GROUP)
    rev_re, rev_im = pw_re[:, chunk - 1::-1][:, :chunk], pw_im[:, chunk - 1::-1][:, :chunk]
    w_re = (rev_re[:, :, None, :] * bb_re.transpose(0, 2, 1)[:, None]
            - rev_im[:, :, None, :] * bb_im.transpose(0, 2, 1)[:, None]).reshape(g, chunk * S5_GROUP, p)
    w_im = (rev_re[:, :, None, :] * bb_im.transpose(0, 2, 1)[:, None]
            + rev_im[:, :, None, :] * bb_re.transpose(0, 2, 1)[:, None]).reshape(g, chunk * S5_GROUP, p)
    v_re = ca_re[:, 1:].transpose(0, 3, 1, 2).reshape(g, p, chunk * S5_GROUP)
    v_im = -ca_im[:, 1:].transpose(0, 3, 1, 2).reshape(g, p, chunk * S5_GROUP)
    sr, si = [pw_re[:, chunk]], [pw_im[:, chunk]]
    n_steps = max(1, int(math.log2(n_chunks))) if n_chunks > 1 else 1
    for _ in range(n_steps - 1):
        r0, i0 = sr[-1], si[-1]
        sr.append(r0 * r0 - i0 * i0)
        si.append(2.0 * r0 * i0)
    return (m.astype(BF16), w_re.astype(BF16), w_im.astype(BF16), v_re.astype(BF16), v_im.astype(BF16),
            jnp.stack(sr, 1), jnp.stack(si, 1))


def _s5_kernel(u_ref, m_ref, wre_ref, wim_ref, vre_ref, vim_ref, are_ref, aim_ref, h0re_ref, h0im_ref,
               y_ref, hre_ref, him_ref, *, n_chunks, has_h0):
    u = u_ref[0]
    u_hi, u_lo = _split2(u)
    y = _dot(u_hi, m_ref[0])
    s_re = _dot(u_hi, wre_ref[0]) + _dot(u_lo, wre_ref[0])
    s_im = _dot(u_hi, wim_ref[0]) + _dot(u_lo, wim_ref[0])
    pw_re, pw_im = are_ref[0], aim_ref[0]
    if has_h0:
        h0r, h0i = h0re_ref[0], h0im_ref[0]
        ar, ai = pw_re[0:1], pw_im[0:1]
        s_re = s_re + ar * h0r - ai * h0i
        s_im = s_im + ar * h0i + ai * h0r
        hs_re, hs_im = h0r, h0i
    else:
        row = lax.broadcasted_iota(jnp.int32, s_re.shape, 0) % n_chunks
        shift, j = 1, 0
        while shift < n_chunks:
            ar, ai = pw_re[j:j + 1], pw_im[j:j + 1]
            keep = row >= shift
            pr = jnp.where(keep, pltpu.roll(s_re, shift, 0), 0.0)
            pi = jnp.where(keep, pltpu.roll(s_im, shift, 0), 0.0)
            s_re, s_im = s_re + ar * pr - ai * pi, s_im + ar * pi + ai * pr
            shift, j = shift * 2, j + 1
        keep = row >= 1
        hs_re = jnp.where(keep, pltpu.roll(s_re, 1, 0), 0.0)
        hs_im = jnp.where(keep, pltpu.roll(s_im, 1, 0), 0.0)
    y = y + _dot(hs_re.astype(BF16), vre_ref[0]) + _dot(hs_im.astype(BF16), vim_ref[0])
    y_ref[0] = y
    hre_ref[0] = s_re
    him_ref[0] = s_im


def _s5(u_g, ops, h0_re, h0_im, n_chunks):
    g, r, lc = u_g.shape
    m, w_re, w_im, v_re, v_im, a_re, a_im = ops
    p = w_re.shape[2]
    has_h0 = h0_re is not None
    if not has_h0:
        h0_re = h0_im = jnp.zeros((g, 8, p), F32)
    ns = a_re.shape[1]
    grp = lambda *shape: pl.BlockSpec((1,) + shape, lambda i: (i, 0, 0))
    return pl.pallas_call(
        functools.partial(_s5_kernel, n_chunks=n_chunks, has_h0=has_h0),
        grid=(g,),
        in_specs=[grp(r, lc), grp(lc, lc), grp(lc, p), grp(lc, p), grp(p, lc), grp(p, lc), grp(ns, p), grp(ns, p),
                  grp(h0_re.shape[1], p), grp(h0_re.shape[1], p)],
        out_specs=[grp(r, lc), grp(r, p), grp(r, p)],
        out_shape=[jax.ShapeDtypeStruct((g, r, lc), F32), jax.ShapeDtypeStruct((g, r, p), F32),
                   jax.ShapeDtypeStruct((g, r, p), F32)],
        compiler_params=_cparams(("parallel",)),
        name="s5",
    )(u_g, m, w_re, w_im, v_re, v_im, a_re, a_im, h0_re, h0_im)


def _gla_constants(tc):
    nl = int(math.log2(tc))
    t = np.arange(tc)
    blocks = []
    for j in range(nl):
        m = 1 << j
        base = t & ~(2 * m - 1)
        ref = base + m - 1
        col = np.arange(tc)[None, :]
        is_q = ((t >> j) & 1) == 1
        dq = (col > ref[:, None]) & (col <= t[:, None])
        dk = (col > t[:, None]) & (col <= ref[:, None])
        blocks.append(np.where(is_q[:, None], dq, dk))
    blocks.append(np.tril(np.ones((tc, tc), bool)))
    blocks.append(np.triu(np.ones((tc, tc), bool), 1))
    dall = np.concatenate(blocks, 0).astype(np.float32)
    x = t[:, None] ^ t[None, :]
    lvl = np.where(x > 0, np.floor(np.log2(np.maximum(x, 1))).astype(np.int32), -1)
    lvl = np.where(t[:, None] > t[None, :], lvl, np.where(t[:, None] == t[None, :], -1, -2)).astype(np.int32)
    return jnp.asarray(dall, BF16), jnp.asarray(lvl)


def _gla_kernel(q_ref, k_ref, v_ref, g_ref, r_ref, wgk_ref, bgk_ref, gn_ref, s0_ref, dall_ref, lvl_ref,
                o_ref, sfin_ref, st_ref, *, tc, valid):
    c = pl.program_id(1)
    nl = int(math.log2(tc))
    hdk = GLA_HEADS * GLA_DK

    @pl.when(c == 0)
    def _():
        st_ref[...] = s0_ref[0]

    q = q_ref[0] * (GLA_DK ** -0.5)
    k = k_ref[0]
    v = v_ref[0]
    x = _dot(r_ref[0].astype(BF16), wgk_ref[...]) + bgk_ref[...]
    gk = _log_sigmoid_pair(x)[0] / GLA_GATE_NORM
    if valid < tc:
        live = lax.broadcasted_iota(jnp.int32, gk.shape, 0) < valid
        gk = jnp.where(live, gk, 0.0)
        k = jnp.where(live, k, 0.0)
    g3 = jnp.concatenate(_split3(gk), axis=1)
    r3 = _dot(dall_ref[...], g3)
    ex = jnp.exp(r3[:, :hdk] + r3[:, hdk:2 * hdk] + r3[:, 2 * hdk:])
    e_cum = ex[nl * tc:(nl + 1) * tc]
    e_rem = ex[(nl + 1) * tc:]
    bl = _dot_tn(g3, jnp.ones((tc, GLA_DV), BF16))
    decay = jnp.exp(bl[:hdk] + bl[hdk:2 * hdk] + bl[2 * hdk:])

    row = lax.broadcasted_iota(jnp.int32, q.shape, 0)
    lane = lax.broadcasted_iota(jnp.int32, (tc, 2 * GLA_DK), 1)
    lvl = lvl_ref[...]
    q_in = (q * e_cum).astype(BF16)
    k_out = (k * e_rem).astype(BF16)
    xs = []
    for j in range(nl):
        is_q = ((row >> j) & 1) == 1
        xs.append((jnp.where(is_q, q, k) * ex[j * tc:(j + 1) * tc]).astype(BF16))
    qb, kb = q.astype(BF16), k.astype(BF16)
    gn = gn_ref[...]
    for h in range(GLA_HEADS):
        pair, sub = h // 2, h % 2
        ps = slice(pair * 2 * GLA_DK, (pair + 1) * 2 * GLA_DK)
        in_head = (lane < GLA_DK) if sub == 0 else (lane >= GLA_DK)
        att = jnp.where(lvl == -1, _dot_nt(jnp.where(in_head, qb[:, ps], 0), kb[:, ps]), 0.0)
        for j in range(nl):
            xp = xs[j][:, ps]
            att = jnp.where(lvl == j, _dot_nt(jnp.where(in_head, xp, 0), xp), att)
        vh = v[:, h * GLA_DV:(h + 1) * GLA_DV].astype(BF16)
        st = st_ref[pair]
        o = _dot(att.astype(BF16), vh) + _dot(jnp.where(in_head, q_in[:, ps], 0), st.astype(BF16))
        upd = _dot_tn(k_out[:, ps], vh)
        rsel = lax.broadcasted_iota(jnp.int32, upd.shape, 0)
        mine = (rsel < GLA_DK) if sub == 0 else (rsel >= GLA_DK)
        st_ref[pair] = jnp.where(mine, decay[ps] * st + upd, st)
        o = o * lax.rsqrt(jnp.mean(o * o, axis=-1, keepdims=True) + EPS) * gn
        gh = g_ref[0, :, h * GLA_DV:(h + 1) * GLA_DV]
        o_ref[0, :, h * GLA_DV:(h + 1) * GLA_DV] = o * (gh * jax.nn.sigmoid(gh))

    sfin_ref[0] = st_ref[...]


def _gla(p3, w_gk, b_gk, gla_norm, s0, tc, valid):
    b, s, _ = p3.shape
    hdk, hdv = GLA_HEADS * GLA_DK, GLA_HEADS * GLA_DV
    u_w = 2 * hdk
    dall, lvl = _gla_constants(tc)
    const = lambda shape: pl.BlockSpec(shape, lambda i, j: tuple(0 for _ in shape))
    return pl.pallas_call(
        functools.partial(_gla_kernel, tc=tc, valid=valid),
        grid=(b, s // tc),
        in_specs=[pl.BlockSpec((1, tc, hdk), lambda i, j: (i, j, u_w // hdk)),
                  pl.BlockSpec((1, tc, hdk), lambda i, j: (i, j, u_w // hdk + 1)),
                  pl.BlockSpec((1, tc, hdv), lambda i, j: (i, j, (u_w + 2 * hdk) // hdv)),
                  pl.BlockSpec((1, tc, hdv), lambda i, j: (i, j, (u_w + 2 * hdk) // hdv + 1)),
                  pl.BlockSpec((1, tc, LANES), lambda i, j: (i, j, (u_w + 2 * hdk + 2 * hdv) // LANES)),
                  const((LANES, hdk)), const((1, hdk)), const((1, GLA_DV)),
                  pl.BlockSpec((1, GLA_HEADS // 2, 2 * GLA_DK, GLA_DV), lambda i, j: (i, 0, 0, 0)),
                  const(dall.shape), const(lvl.shape)],
        out_specs=[pl.BlockSpec((1, tc, hdv), lambda i, j: (i, j, 0)),
                   pl.BlockSpec((1, GLA_HEADS // 2, 2 * GLA_DK, GLA_DV), lambda i, j: (i, 0, 0, 0))],
        out_shape=[jax.ShapeDtypeStruct((b, s, hdv), F32),
                   jax.ShapeDtypeStruct((b, GLA_HEADS // 2, 2 * GLA_DK, GLA_DV), F32)],
        scratch_shapes=[pltpu.VMEM((GLA_HEADS // 2, 2 * GLA_DK, GLA_DV), F32)],
        compiler_params=_cparams(("parallel", "arbitrary")),
        name="gla",
    )(p3, p3, p3, p3, p3, w_gk, b_gk, gla_norm, s0, dall, lvl)


def _mix_out_kernel(x_ref, yc_ref, u_ref, og_ref, d_ref, wglu_ref, bglu_ref, wtop_ref, wbot_ref, o_ref):
    y = yc_ref[...] + d_ref[...] * u_ref[...]
    z = jax.nn.gelu(y)
    s5 = z * jax.nn.sigmoid(_dot(z.astype(BF16), wglu_ref[...]) + bglu_ref[...])
    o_ref[...] = (x_ref[...] + _dot(s5.astype(BF16), wtop_ref[...])
                  + _dot(og_ref[...].astype(BF16), wbot_ref[...]))


def _mix_out(x, y_conv, p, o_gla, d_skip, w_glu, b_glu, w_top, w_bot):
    t, d = x.shape
    w5 = y_conv.shape[1]
    tm = _tile(t, 512)
    row = lambda w: pl.BlockSpec((tm, w), lambda i: (i, 0))
    const = lambda a, b: pl.BlockSpec((a, b), lambda i: (0, 0))
    return pl.pallas_call(
        _mix_out_kernel,
        grid=(t // tm,),
        in_specs=[row(d), row(w5), row(w5), row(o_gla.shape[1]), const(1, w5), const(w5, w5), const(1, w5),
                  const(w5, d), const(o_gla.shape[1], d)],
        out_specs=row(d),
        out_shape=jax.ShapeDtypeStruct((t, d), F32),
        compiler_params=_cparams(("parallel",)),
        name="mix_out",
    )(x, y_conv, p, o_gla, d_skip, w_glu, b_glu, w_top, w_bot)


def _ffn_kernel(x_ref, gn_ref, hist_ref, wu_ref, wg_ref, wc_ref, bc_ref, wd_ref, gf_ref,
                o_ref, hist_out_ref, h_scr, acc_scr, pad_scr, carry_scr, *, rs, hist_rows, final_norm):
    t = pl.program_id(1)
    c = pl.program_id(2)
    tm = x_ref.shape[1]

    @pl.when(c == 0)
    def _():
        h_scr[...] = _rmsnorm(x_ref[0], gn_ref[...]).astype(BF16)
        acc_scr[...] = jnp.zeros_like(acc_scr)

    @pl.when(t == 0)
    def _():
        carry_scr[c] = hist_ref[0]

    h = h_scr[...]
    u = _dot(h, wu_ref[...])
    g = _dot(h, wg_ref[...])
    pad_scr[0:hist_rows] = carry_scr[c]
    pad_scr[hist_rows:] = u
    wc = wc_ref[...]
    uc = bc_ref[...]
    for j in range(CONV_W):
        off = hist_rows - (CONV_W - 1 - j) * rs
        uc = uc + wc[j:j + 1] * pad_scr[off:off + tm]
    new_hist = pad_scr[tm:tm + hist_rows]
    carry_scr[c] = new_hist
    hist_out_ref[0, 0] = new_hist
    hdn = jax.nn.gelu(uc) * g
    acc_scr[...] += _dot(hdn.astype(BF16), wd_ref[...])

    @pl.when(c == pl.num_programs(2) - 1)
    def _():
        y = x_ref[0] + acc_scr[...]
        if final_norm:
            y = _rmsnorm(y, gf_ref[...])
        o_ref[0] = y


def _ffn(x3, g_norm, hist, w_up, w_conv, b_conv, w_down, g_final, rs, final_norm):
    nb, s, d = x3.shape
    dff = w_down.shape[0]
    hist_rows = hist.shape[1]
    tm = _tile(s, 512)
    nt = s // tm
    ck = dff // 2 if (dff // 2) % LANES == 0 else dff
    nck = dff // ck
    y, hist_all = pl.pallas_call(
        functools.partial(_ffn_kernel, rs=rs, hist_rows=hist_rows, final_norm=final_norm),
        grid=(nb, s // tm, nck),
        in_specs=[pl.BlockSpec((1, tm, d), lambda b, t, c: (b, t, 0)),
                  pl.BlockSpec((1, d), lambda b, t, c: (0, 0)),
                  pl.BlockSpec((1, hist_rows, ck), lambda b, t, c: (b, 0, c)),
                  pl.BlockSpec((d, ck), lambda b, t, c: (0, c)),
                  pl.BlockSpec((d, ck), lambda b, t, c: (0, nck + c)),
                  pl.BlockSpec((CONV_W, ck), lambda b, t, c: (0, c)),
                  pl.BlockSpec((1, ck), lambda b, t, c: (0, c)),
                  pl.BlockSpec((ck, d), lambda b, t, c: (c, 0)),
                  pl.BlockSpec((1, d), lambda b, t, c: (0, 0))],
        out_specs=[pl.BlockSpec((1, tm, d), lambda b, t, c: (b, t, 0)),
                   pl.BlockSpec((1, 1, hist_rows, ck), lambda b, t, c: (b, t, 0, c))],
        out_shape=[jax.ShapeDtypeStruct((nb, s, d), F32), jax.ShapeDtypeStruct((nb, nt, hist_rows, dff), F32)],
        scratch_shapes=[pltpu.VMEM((tm, d), BF16), pltpu.VMEM((tm, d), F32),
                        pltpu.VMEM((hist_rows + tm, ck), F32), pltpu.VMEM((nck, hist_rows, ck), F32)],
        compiler_params=_cparams(("parallel", "arbitrary", "arbitrary")),
        name="ffn",
    )(x3, g_norm, hist, w_up, w_up, w_conv, b_conv, w_down, g_final)
    return y, hist_all[:, nt - 1]


def _sb_scores(z, cb, mo, mask):
    tk = z.shape[1]
    lon = jnp.maximum(z, 0.0) + jnp.log2(1.0 + jnp.exp2(-jnp.abs(z)))
    lsp = z - lon
    if mask is not None:
        lon = jnp.where(mask, lon, 0.0)
    r = _dot(lon.astype(BF16), mo)
    a = jnp.exp2(lsp + r[:, :tk] + jnp.tile(cb, (1, tk // LANES)))
    if mask is not None:
        a = jnp.where(mask, a, 0.0)
    return a.astype(BF16), cb + r[:, tk:]


def _sb_mo(tk):
    j = np.arange(tk)[:, None]
    s = np.arange(tk)[None, :]
    return jnp.asarray(-np.concatenate([(j > s), np.ones((tk, LANES), bool)], 1).astype(np.float32), BF16)


def _sb_prompt_kernel(bias_ref, q_ref, kt_ref, vt_ref, mo_ref, o_ref, acc_ref, cb_ref, *, tq):
    hp = pl.program_id(1)
    qi = pl.program_id(2)
    q = q_ref[0]
    lane = lax.broadcasted_iota(jnp.int32, q.shape, 1)
    tri = lax.broadcasted_iota(jnp.int32, (tq, tq), 1) < lax.broadcasted_iota(jnp.int32, (tq, tq), 0)
    mo = mo_ref[...]
    qh = [jnp.where(lane < SB_DIM, q, 0), jnp.where(lane >= SB_DIM, q, 0)]
    bias = [bias_ref[2 * hp], bias_ref[2 * hp + 1]]

    def tiles(kbs, mask, first):
        for sub in range(2):
            cb = jnp.zeros((tq, LANES), F32) if first else cb_ref[sub]
            acc = None if first else acc_ref[sub]
            for kb in kbs:
                z = _dot(qh[sub], kt_ref[0, kb]) + bias[sub]
                a, cb = _sb_scores(z, cb, mo, mask)
                pv = _dot_nt(a, vt_ref[0, kb])
                acc = pv if acc is None else acc + pv
            acc_ref[sub] = acc
            cb_ref[sub] = cb

    tiles([qi], tri, True)

    def body(i, carry):
        kb = qi - 1 - 2 * i
        tiles([kb, kb - 1], None, False)
        return carry

    lax.fori_loop(0, qi // 2, body, 0)

    @pl.when(qi % 2 == 1)
    def _():
        tiles([0], None, False)

    o_ref[0] = jnp.where(lane < SB_DIM, acc_ref[0], acc_ref[1])


def _sb_prompt(q_b, kt_b, vt_b, bias2):
    b, s, hd = q_b.shape
    tq = kt_b.shape[3]
    nt = s // tq
    return pl.pallas_call(
        functools.partial(_sb_prompt_kernel, tq=tq),
        grid=(b, hd // LANES, nt),
        in_specs=[pl.BlockSpec(memory_space=pltpu.SMEM),
                  pl.BlockSpec((1, tq, LANES), lambda i, h, j: (i, j, h)),
                  pl.BlockSpec((1, nt, LANES, tq), lambda i, h, j: (i, 0, h, 0)),
                  pl.BlockSpec((1, nt, LANES, tq), lambda i, h, j: (i, 0, h, 0)),
                  pl.BlockSpec((tq, tq + LANES), lambda i, h, j: (0, 0))],
        out_specs=pl.BlockSpec((1, tq, LANES), lambda i, h, j: (i, j, h)),
        out_shape=jax.ShapeDtypeStruct((b, s, hd), F32),
        scratch_shapes=[pltpu.VMEM((2, tq, LANES), F32), pltpu.VMEM((2, tq, LANES), F32)],
        compiler_params=_cparams(("parallel", "parallel", "arbitrary")),
        name="sb_prompt",
    )(bias2, q_b, kt_b, vt_b, _sb_mo(tq))


SB_PAGES_PER_STEP = 4


def _sb_sample_kernel(pt_ref, qbd_ref, bias_ref, kn_ref, vn_ref, *rest, nq, npp):
    ck_refs, cv_refs = rest[:npp], rest[npp:2 * npp]
    mo_ref, o_ref, acc_ref, cb_ref = rest[2 * npp:]
    j = pl.program_id(1)
    qbd = qbd_ref[0]
    bias = bias_ref[...]
    mo = mo_ref[...]
    nh = SB_HEADS

    @pl.when(j == 0)
    def _():
        z = _dot_nt(qbd, kn_ref[0].astype(BF16)) + bias
        s_idx = lax.broadcasted_iota(jnp.int32, z.shape, 1)
        q_idx = lax.broadcasted_iota(jnp.int32, z.shape, 0) // nh
        a, cb = _sb_scores(z, jnp.zeros((z.shape[0], LANES), F32), mo, s_idx < q_idx)
        acc_ref[...] = _dot(a, vn_ref[0].astype(BF16))
        cb_ref[...] = cb

    cb = cb_ref[...]
    acc = acc_ref[...]
    for i in range(npp):
        z = _dot(qbd, ck_refs[i][0, 0].astype(BF16)) + bias
        a, cb = _sb_scores(z, cb, mo, None)
        acc = acc + _dot_nt(a, cv_refs[i][0, 0].astype(BF16))
    acc_ref[...] = acc
    cb_ref[...] = cb

    @pl.when(j == pl.num_programs(1) - 1)
    def _():
        r_h = lax.broadcasted_iota(jnp.int32, acc.shape, 0) % nh
        l_h = lax.broadcasted_iota(jnp.int32, acc.shape, 1) // SB_DIM
        sel = jnp.where(r_h == l_h, acc, 0.0)
        o_ref[0] = jnp.sum(sel.reshape(nq, nh, acc.shape[1]), axis=1)


def _sb_sample(q_b, k_new, v_new, cache_kt, cache_vt, layer, page_table, bias2):
    b, nq, hd = q_b.shape
    nh = SB_HEADS
    n_pages = page_table.shape[1]
    page = cache_kt.shape[3]
    npp = SB_PAGES_PER_STEP if n_pages % SB_PAGES_PER_STEP == 0 else 1
    qs = q_b.reshape(b, nq, nh, SB_DIM)
    eye = jnp.eye(nh, dtype=BF16)
    qbd = (qs[:, :, :, None, :] * eye[None, None, :, :, None]).reshape(b, nq * nh, hd)
    bias_b = jnp.broadcast_to(jnp.tile(bias2, nq)[:, None], (nq * nh, page))
    pad = lambda a: jnp.pad(a, ((0, 0), (0, page - nq), (0, 0)))
    page_spec = lambda i_page: pl.BlockSpec(
        (1, 1, hd, page), lambda i, j, pt: (layer, pt[i, n_pages - 1 - (j * npp + i_page)], 0, 0))
    return pl.pallas_call(
        functools.partial(_sb_sample_kernel, nq=nq, npp=npp),
        grid_spec=pltpu.PrefetchScalarGridSpec(
            num_scalar_prefetch=1,
            grid=(b, n_pages // npp),
            in_specs=([pl.BlockSpec((1, nq * nh, hd), lambda i, j, pt: (i, 0, 0)),
                       pl.BlockSpec((nq * nh, page), lambda i, j, pt: (0, 0)),
                       pl.BlockSpec((1, page, hd), lambda i, j, pt: (i, 0, 0)),
                       pl.BlockSpec((1, page, hd), lambda i, j, pt: (i, 0, 0))]
                      + [page_spec(i) for i in range(npp)] + [page_spec(i) for i in range(npp)]
                      + [pl.BlockSpec((page, page + LANES), lambda i, j, pt: (0, 0))]),
            out_specs=pl.BlockSpec((1, nq, hd), lambda i, j, pt: (i, 0, 0)),
            scratch_shapes=[pltpu.VMEM((nq * nh, hd), F32), pltpu.VMEM((nq * nh, LANES), F32)],
        ),
        out_shape=jax.ShapeDtypeStruct((b, nq, hd), F32),
        compiler_params=_cparams(("parallel", "arbitrary")),
        name="sb_sample",
    )(page_table, qbd, bias_b, pad(k_new), pad(v_new), *([cache_kt] * npp), *([cache_vt] * npp), _sb_mo(page))


def _even_layer(x3, g_mix, w_in_p, s5_ops, d_skip, w_glu, b_glu, w_gk_p, b_gk, gla_norm, w_top, w_bot,
                s5_h0, gla_s0, chunk):
    b, s, d = x3.shape
    s5_w = d_skip.shape[1]
    ng = s5_w // S5_GROUP
    nc = s // chunk
    x2 = x3.reshape(b * s, d)
    p = _inproj(x2, g_mix, w_in_p)
    u_g = (p[:, :s5_w].reshape(b, nc, chunk, ng, S5_GROUP).transpose(3, 0, 1, 2, 4)
           .reshape(ng, b * nc, chunk * S5_GROUP))
    if s5_h0 is None:
        h0r = h0i = None
    else:
        h0r, h0i = s5_h0[0].transpose(1, 0, 2), s5_h0[1].transpose(1, 0, 2)
    y_g, hre, him = _s5(u_g, s5_ops, h0r, h0i, nc)
    y_conv = (y_g.reshape(ng, b, nc, chunk, S5_GROUP).transpose(1, 2, 3, 0, 4).reshape(b * s, s5_w))
    p_state = hre.shape[2]
    fin = lambda a: a.reshape(ng, b, nc, p_state)[:, :, -1].transpose(1, 0, 2)
    p3 = p.reshape(b, s, p.shape[1])
    gla_tc = _tile(s, 128) if s >= 16 else 16
    gla_valid = min(s, gla_tc)
    sp = p3 if s >= gla_tc else jnp.pad(p3, ((0, 0), (0, gla_tc - s), (0, 0)))
    s0 = (jnp.zeros((b, GLA_HEADS // 2, 2 * GLA_DK, GLA_DV), F32) if gla_s0 is None
          else gla_s0.reshape(b, GLA_HEADS // 2, 2 * GLA_DK, GLA_DV))
    o_gla, s_fin = _gla(sp, w_gk_p, b_gk, gla_norm, s0, gla_tc, gla_valid)
    o_gla = o_gla[:, :s].reshape(b * s, o_gla.shape[2])
    x2 = _mix_out(x2, y_conv, p, o_gla, d_skip, w_glu, b_glu, w_top, w_bot)
    return x2.reshape(b, s, d), fin(hre), fin(him), s_fin.reshape(b, GLA_HEADS, GLA_DK, GLA_DV)


def kernel(x_prompt, x_sample, state_s5_re, state_s5_im, state_gla, cache_k, cache_v, state_ffn_conv, page_table, norm_mix, norm_ffn, norm_final, w_in0, s5_a_re, s5_a_im, s5_log_dt, s5_b_re, s5_b_im, s5_c_re, s5_c_im, s5_d, s5_w_glu, s5_b_glu, gla_w_gk, gla_b_gk, gla_norm, w_out0, w_qkv1, w_out1, sb_bias, ffn_w_up, ffn_w_conv, ffn_b_conv, ffn_w_down):
    bp, sp, d = x_prompt.shape
    bs, ss, _ = x_sample.shape
    depth = norm_mix.shape[0]
    dff = ffn_w_down.shape[1]
    s5_w = s5_d.shape[1]
    xp, xs = x_prompt, x_sample
    outs_p = {k: [] for k in ("s5r", "s5i", "gla", "k", "v", "conv")}
    outs_s = {k: [] for k in ("s5r", "s5i", "gla", "k", "v", "conv")}
    hist_rows_p = 8
    for l in range(depth):
        i = l // 2
        g_mix = norm_mix[l][None, :]
        if l % 2 == 0:
            in0 = w_in0.shape[2]
            in0p = -(-in0 // LANES) * LANES
            w_in_p = jnp.pad(w_in0[i], ((0, 0), (0, in0p - in0))).astype(BF16)
            w_gk_p = jnp.pad(gla_w_gk[i], ((0, LANES - GLA_RANK), (0, 0))).astype(BF16)
            common = (s5_d[i][None, :], s5_w_glu[i].astype(BF16), s5_b_glu[i][None, :], w_gk_p,
                      gla_b_gk[i][None, :], gla_norm[i][None, :], w_out0[i][:s5_w].astype(BF16),
                      w_out0[i][s5_w:].astype(BF16))
            s5_args = (s5_a_re[i], s5_a_im[i], s5_log_dt[i], s5_b_re[i], s5_b_im[i], s5_c_re[i], s5_c_im[i])
            chunk_p = _tile(sp, S5_CHUNK)
            ops_p = _s5_operators(*s5_args, chunk_p, sp // chunk_p)
            ops_s = _s5_operators(*s5_args, ss, 1)
            xp, hr, hi, sg = _even_layer(xp, g_mix, w_in_p, ops_p, *common, None, None, chunk_p)
            xs, hr2, hi2, sg2 = _even_layer(xs, g_mix, w_in_p, ops_s, *common,
                                            (state_s5_re[i], state_s5_im[i]), state_gla[i], ss)
            for o, vals in ((outs_p, (hr, hi, sg)), (outs_s, (hr2, hi2, sg2))):
                o["s5r"].append(vals[0])
                o["s5i"].append(vals[1])
                o["gla"].append(vals[2])
        else:
            w_qkv = w_qkv1[i].astype(BF16)
            w_out = w_out1[i].astype(BF16)
            bias2 = sb_bias[i] * LOG2E
            qb, kt, vt, ktb, vtb = _qkv_t(xp, g_mix, w_qkv[:, :d], w_qkv[:, d:].T, _tile(sp, 256))
            att = _sb_prompt(qb, ktb, vtb, bias2)
            xp = _proj_res(xp.reshape(bp * sp, d), att.reshape(bp * sp, d), w_out).reshape(bp, sp, d)
            qsb, ks, vs, _, _ = _qkv(xs.reshape(bs * ss, d), g_mix, w_qkv)
            as_t = lambda c: c.transpose(0, 1, 3, 4, 2).reshape(c.shape[0], c.shape[1], d, c.shape[2])
            att_s = _sb_sample(qsb.reshape(bs, ss, d), ks.reshape(bs, ss, d), vs.reshape(bs, ss, d),
                               as_t(cache_k), as_t(cache_v), i, page_table, bias2)
            xs = _proj_res(xs.reshape(bs * ss, d), att_s.reshape(bs * ss, d), w_out).reshape(bs, ss, d)
            outs_p["k"].append(kt.reshape(bp, SB_HEADS, SB_DIM, sp).transpose(0, 3, 1, 2))
            outs_p["v"].append(vt.reshape(bp, SB_HEADS, SB_DIM, sp).transpose(0, 3, 1, 2))
            outs_s["k"].append(ks.reshape(bs, ss, SB_HEADS, SB_DIM))
            outs_s["v"].append(vs.reshape(bs, ss, SB_HEADS, SB_DIM))
        final = l == depth - 1
        w_up = ffn_w_up[l].astype(BF16)
        w_down = ffn_w_down[l].astype(BF16)
        ffn_args = (w_up, ffn_w_conv[l], ffn_b_conv[l][None, :], w_down, norm_final[None, :])
        xp, hist_p = _ffn(xp, norm_ffn[l][None, :], jnp.zeros((bp, hist_rows_p, dff), F32), *ffn_args, 1, final)
        outs_p["conv"].append(hist_p[:, hist_rows_p - (CONV_W - 1):])
        hist_s = state_ffn_conv[l].transpose(1, 0, 2).reshape(1, (CONV_W - 1) * bs, dff)
        xs_t, hist_s = _ffn(xs.transpose(1, 0, 2).reshape(1, ss * bs, d), norm_ffn[l][None, :], hist_s,
                            *ffn_args, bs, final)
        xs = xs_t.reshape(ss, bs, d).transpose(1, 0, 2)
        outs_s["conv"].append(hist_s.reshape(CONV_W - 1, bs, dff).transpose(1, 0, 2))
    st = jnp.stack
    return (xp, xs,
            st(outs_p["s5r"]), st(outs_p["s5i"]), st(outs_p["gla"]), st(outs_p["k"]), st(outs_p["v"]), st(outs_p["conv"]),
            st(outs_s["s5r"]), st(outs_s["s5i"]), st(outs_s["gla"]), st(outs_s["k"]), st(outs_s["v"]), st(outs_s["conv"]))
```

```python
import functools
import math

import jax
import jax.numpy as jnp
import numpy as np
from jax import lax
from jax.experimental import pallas as pl
from jax.experimental.pallas import tpu as pltpu

F32 = jnp.float32
BF16 = jnp.bfloat16
EPS = 1e-6

S5_GROUP = 16
S5_CHUNK = 16
GLA_HEADS = 4
GLA_DK = 64
GLA_DV = 128
GLA_RANK = 16
GLA_GATE_NORM = 16.0
SB_HEADS = 16
SB_DIM = 64
CONV_W = 3
LOG2E = math.log2(math.e)
SB_QSCALE = SB_DIM ** -0.5 * LOG2E
LANES = 128
VMEM_LIMIT = 56 * 1024 * 1024


def _cparams(sem):
    return pltpu.CompilerParams(dimension_semantics=sem, vmem_limit_bytes=VMEM_LIMIT)


def _tile(n, pref):
    if n <= pref:
        return n
    t = pref
    while n % t:
        t //= 2
    return t


def _rmsnorm(x, g):
    y = x * lax.rsqrt(jnp.mean(x * x, axis=-1, keepdims=True) + EPS)
    return y * g


def _split2(x):
    hi = x.astype(BF16)
    lo = (x - hi.astype(F32)).astype(BF16)
    return hi, lo


def _split3(x):
    hi = x.astype(BF16)
    r = x - hi.astype(F32)
    mid = r.astype(BF16)
    lo = (r - mid.astype(F32)).astype(BF16)
    return hi, mid, lo


def _dot(a, b):
    return jnp.dot(a, b, preferred_element_type=F32)


def _dot_nt(a, b):
    return lax.dot_general(a, b, (((1,), (1,)), ((), ())), preferred_element_type=F32)


def _dot_tn(a, b):
    return lax.dot_general(a, b, (((0,), (0,)), ((), ())), preferred_element_type=F32)


def _log_sigmoid_pair(z):
    lsp = jnp.minimum(z, 0.0) - jnp.log1p(jnp.exp(-jnp.abs(z)))
    return lsp, lsp - z


def _inproj_kernel(x_ref, g_ref, w_ref, o_ref):
    h = _rmsnorm(x_ref[...], g_ref[...]).astype(BF16)
    o_ref[...] = _dot(h, w_ref[...])


def _inproj(x, g, w):
    t, d = x.shape
    n = w.shape[1]
    tm = _tile(t, 512)
    return pl.pallas_call(
        _inproj_kernel,
        grid=(t // tm,),
        in_specs=[pl.BlockSpec((tm, d), lambda i: (i, 0)),
                  pl.BlockSpec((1, d), lambda i: (0, 0)),
                  pl.BlockSpec((d, n), lambda i: (0, 0))],
        out_specs=pl.BlockSpec((tm, n), lambda i: (i, 0)),
        out_shape=jax.ShapeDtypeStruct((t, n), F32),
        compiler_params=_cparams(("parallel",)),
        name="inproj",
    )(x, g, w)


def _qkv_kernel(x_ref, g_ref, w_ref, qb_ref, k_ref, v_ref, kb_ref, vb_ref):
    d = x_ref.shape[1]
    h = _rmsnorm(x_ref[...], g_ref[...]).astype(BF16)
    p = _dot(h, w_ref[...])
    q, k, v = p[:, :d], p[:, d:2 * d], p[:, 2 * d:]
    qb_ref[...] = (q * SB_QSCALE).astype(BF16)
    k_ref[...] = k
    v_ref[...] = v
    kb_ref[...] = k.astype(BF16)
    vb_ref[...] = v.astype(BF16)


def _qkv(x, g, w):
    t, d = x.shape
    tm = _tile(t, 256)
    row = pl.BlockSpec((tm, d), lambda i: (i, 0))
    return pl.pallas_call(
        _qkv_kernel,
        grid=(t // tm,),
        in_specs=[row, pl.BlockSpec((1, d), lambda i: (0, 0)), pl.BlockSpec((d, 3 * d), lambda i: (0, 0))],
        out_specs=[row, row, row, row, row],
        out_shape=[jax.ShapeDtypeStruct((t, d), BF16), jax.ShapeDtypeStruct((t, d), F32),
                   jax.ShapeDtypeStruct((t, d), F32), jax.ShapeDtypeStruct((t, d), BF16),
                   jax.ShapeDtypeStruct((t, d), BF16)],
        compiler_params=_cparams(("parallel",)),
        name="qkv",
    )(x, g, w)


def _qkv_t_kernel(x_ref, g_ref, wq_ref, wkvt_ref, qb_ref, kt_ref, vt_ref, ktb_ref, vtb_ref):
    d = x_ref.shape[2]
    h = _rmsnorm(x_ref[0], g_ref[...]).astype(BF16)
    qb_ref[0] = (_dot(h, wq_ref[...]) * SB_QSCALE).astype(BF16)
    kvt = _dot_nt(wkvt_ref[...], h)
    kt_ref[0] = kvt[:d]
    vt_ref[0] = kvt[d:]
    ktb_ref[0, 0] = kvt[:d].astype(BF16)
    vtb_ref[0, 0] = kvt[d:].astype(BF16)


def _qkv_t(x3, g, wq, wkvt, tm):
    b, s, d = x3.shape
    nt = s // tm
    return pl.pallas_call(
        _qkv_t_kernel,
        grid=(b, nt),
        in_specs=[pl.BlockSpec((1, tm, d), lambda i, t: (i, t, 0)),
                  pl.BlockSpec((1, d), lambda i, t: (0, 0)),
                  pl.BlockSpec((d, d), lambda i, t: (0, 0)),
                  pl.BlockSpec((2 * d, d), lambda i, t: (0, 0))],
        out_specs=[pl.BlockSpec((1, tm, d), lambda i, t: (i, t, 0)),
                   pl.BlockSpec((1, d, tm), lambda i, t: (i, 0, t)),
                   pl.BlockSpec((1, d, tm), lambda i, t: (i, 0, t)),
                   pl.BlockSpec((1, 1, d, tm), lambda i, t: (i, t, 0, 0)),
                   pl.BlockSpec((1, 1, d, tm), lambda i, t: (i, t, 0, 0))],
        out_shape=[jax.ShapeDtypeStruct((b, s, d), BF16), jax.ShapeDtypeStruct((b, d, s), F32),
                   jax.ShapeDtypeStruct((b, d, s), F32), jax.ShapeDtypeStruct((b, nt, d, tm), BF16),
                   jax.ShapeDtypeStruct((b, nt, d, tm), BF16)],
        compiler_params=_cparams(("parallel", "parallel")),
        name="qkv_t",
    )(x3, g, wq, wkvt)


def _proj_res_kernel(x_ref, a_ref, w_ref, o_ref):
    o_ref[...] = x_ref[...] + _dot(a_ref[...].astype(BF16), w_ref[...])


def _proj_res(x, a, w):
    t, d = x.shape
    k = a.shape[1]
    tm = _tile(t, 512)
    return pl.pallas_call(
        _proj_res_kernel,
        grid=(t // tm,),
        in_specs=[pl.BlockSpec((tm, d), lambda i: (i, 0)),
                  pl.BlockSpec((tm, k), lambda i: (i, 0)),
                  pl.BlockSpec((k, d), lambda i: (0, 0))],
        out_specs=pl.BlockSpec((tm, d), lambda i: (i, 0)),
        out_shape=jax.ShapeDtypeStruct((t, d), F32),
        compiler_params=_cparams(("parallel",)),
        name="proj_res",
    )(x, a, w)


def _s5_operators(a_re, a_im, log_dt, b_re, b_im, c_re, c_im, chunk, n_chunks):
    hp = lax.Precision.HIGHEST
    g, p = a_re.shape
    dt = jnp.exp(log_dt)[:, None]
    mag = jnp.exp(a_re * dt)
    ab_re, ab_im = mag * jnp.cos(a_im * dt), mag * jnp.sin(a_im * dt)
    den = a_re * a_re + a_im * a_im
    nr, ni = ab_re - 1.0, ab_im
    coef_re = (nr * a_re + ni * a_im) / den
    coef_im = (ni * a_re - nr * a_im) / den
    bb_re = coef_re[..., None] * b_re - coef_im[..., None] * b_im
    bb_im = coef_re[..., None] * b_im + coef_im[..., None] * b_re
    pr, pi = [jnp.ones_like(ab_re)], [jnp.zeros_like(ab_re)]
    for _ in range(chunk):
        r0, i0 = pr[-1], pi[-1]
        pr.append(r0 * ab_re - i0 * ab_im)
        pi.append(r0 * ab_im + i0 * ab_re)
    pw_re, pw_im = jnp.stack(pr, 1), jnp.stack(pi, 1)
    ca_re = c_re[:, None] * pw_re[:, :, None, :] - c_im[:, None] * pw_im[:, :, None, :]
    ca_im = c_re[:, None] * pw_im[:, :, None, :] + c_im[:, None] * pw_re[:, :, None, :]
    taps = (jnp.einsum('gtcp,gpd->gtcd', ca_re[:, :chunk], bb_re, precision=hp)
            - jnp.einsum('gtcp,gpd->gtcd', ca_im[:, :chunk], bb_im, precision=hp))
    s_idx = np.arange(chunk)[:, None]
    i_idx = np.arange(chunk)[None, :]
    tau = np.clip(i_idx - s_idx, 0, chunk - 1)
    m = taps[:, tau]
    m = jnp.where((i_idx >= s_idx)[None, :, :, None, None], m, 0.0)
    m = m.transpose(0, 1, 4, 2, 3).reshape(g, chunk * S5_GROUP, chunk * S5_GROUP)
    rev_re, rev_im = pw_re[:, chunk - 1::-1][:, :chunk], pw_im[:, chunk - 1::-1][:, :chunk]
    w_re = (rev_re[:, :, None, :] * bb_re.transpose(0, 2, 1)[:, None]
            - rev_im[:, :, None, :] * bb_im.transpose(0, 2, 1)[:, None]).reshape(g, chunk * S5_GROUP, p)
    w_im = (rev_re[:, :, None, :] * bb_im.transpose(0, 2, 1)[:, None]
            + rev_im[:, :, None, :] * bb_re.transpose(0, 2, 1)[:, None]).reshape(g, chunk * S5_GROUP, p)
    v_re = ca_re[:, 1:].transpose(0, 3, 1, 2).reshape(g, p, chunk * S5_GROUP)
    v_im = -ca_im[:, 1:].transpose(0, 3, 1, 2).reshape(g, p, chunk * S5_GROUP)
    sr, si = [pw_re[:, chunk]], [pw_im[:, chunk]]
    n_steps = max(1, int(math.log2(n_chunks))) if n_chunks > 1 else 1
    for _ in range(n_steps - 1):
        r0, i0 = sr[-1], si[-1]
        sr.append(r0 * r0 - i0 * i0)
        si.append(2.0 * r0 * i0)
    return (m.astype(BF16), w_re.astype(BF16), w_im.astype(BF16), v_re.astype(BF16), v_im.astype(BF16),
            jnp.stack(sr, 1), jnp.stack(si, 1))


def _s5_kernel(u_ref, m_ref, wre_ref, wim_ref, vre_ref, vim_ref, are_ref, aim_ref, h0re_ref, h0im_ref,
               y_ref, hre_ref, him_ref, *, n_chunks, has_h0):
    u = u_ref[0]
    u_hi, u_lo = _split2(u)
    y = _dot(u_hi, m_ref[0])
    s_re = _dot(u_hi, wre_ref[0]) + _dot(u_lo, wre_ref[0])
    s_im = _dot(u_hi, wim_ref[0]) + _dot(u_lo, wim_ref[0])
    pw_re, pw_im = are_ref[0], aim_ref[0]
    if has_h0:
        h0r, h0i = h0re_ref[0], h0im_ref[0]
        ar, ai = pw_re[0:1], pw_im[0:1]
        s_re = s_re + ar * h0r - ai * h0i
        s_im = s_im + ar * h0i + ai * h0r
        hs_re, hs_im = h0r, h0i
    else:
        row = lax.broadcasted_iota(jnp.int32, s_re.shape, 0) % n_chunks
        shift, j = 1, 0
        while shift < n_chunks:
            ar, ai = pw_re[j:j + 1], pw_im[j:j + 1]
            keep = row >= shift
            pr = jnp.where(keep, pltpu.roll(s_re, shift, 0), 0.0)
            pi = jnp.where(keep, pltpu.roll(s_im, shift, 0), 0.0)
            s_re, s_im = s_re + ar * pr - ai * pi, s_im + ar * pi + ai * pr
            shift, j = shift * 2, j + 1
        keep = row >= 1
        hs_re = jnp.where(keep, pltpu.roll(s_re, 1, 0), 0.0)
        hs_im = jnp.where(keep, pltpu.roll(s_im, 1, 0), 0.0)
    y = y + _dot(hs_re.astype(BF16), vre_ref[0]) + _dot(hs_im.astype(BF16), vim_ref[0])
    y_ref[0] = y
    hre_ref[0] = s_re
    him_ref[0] = s_im


def _s5(u_g, ops, h0_re, h0_im, n_chunks):
    g, r, lc = u_g.shape
    m, w_re, w_im, v_re, v_im, a_re, a_im = ops
    p = w_re.shape[2]
    has_h0 = h0_re is not None
    if not has_h0:
        h0_re = h0_im = jnp.zeros((g, 8, p), F32)
    ns = a_re.shape[1]
    grp = lambda *shape: pl.BlockSpec((1,) + shape, lambda i: (i, 0, 0))
    return pl.pallas_call(
        functools.partial(_s5_kernel, n_chunks=n_chunks, has_h0=has_h0),
        grid=(g,),
        in_specs=[grp(r, lc), grp(lc, lc), grp(lc, p), grp(lc, p), grp(p, lc), grp(p, lc), grp(ns, p), grp(ns, p),
                  grp(h0_re.shape[1], p), grp(h0_re.shape[1], p)],
        out_specs=[grp(r, lc), grp(r, p), grp(r, p)],
        out_shape=[jax.ShapeDtypeStruct((g, r, lc), F32), jax.ShapeDtypeStruct((g, r, p), F32),
                   jax.ShapeDtypeStruct((g, r, p), F32)],
        compiler_params=_cparams(("parallel",)),
        name="s5",
    )(u_g, m, w_re, w_im, v_re, v_im, a_re, a_im, h0_re, h0_im)


def _gla_constants(tc):
    nl = int(math.log2(tc))
    t = np.arange(tc)
    blocks = []
    for j in range(nl):
        m = 1 << j
        base = t & ~(2 * m - 1)
        ref = base + m - 1
        col = np.arange(tc)[None, :]
        is_q = ((t >> j) & 1) == 1
        dq = (col > ref[:, None]) & (col <= t[:, None])
        dk = (col > t[:, None]) & (col <= ref[:, None])
        blocks.append(np.where(is_q[:, None], dq, dk))
    blocks.append(np.tril(np.ones((tc, tc), bool)))
    blocks.append(np.triu(np.ones((tc, tc), bool), 1))
    dall = np.concatenate(blocks, 0).astype(np.float32)
    x = t[:, None] ^ t[None, :]
    lvl = np.where(x > 0, np.floor(np.log2(np.maximum(x, 1))).astype(np.int32), -1)
    lvl = np.where(t[:, None] > t[None, :], lvl, np.where(t[:, None] == t[None, :], -1, -2)).astype(np.int32)
    return jnp.asarray(dall, BF16), jnp.asarray(lvl)


def _gla_kernel(q_ref, k_ref, v_ref, g_ref, r_ref, wgk_ref, bgk_ref, gn_ref, s0_ref, dall_ref, lvl_ref,
                o_ref, sfin_ref, st_ref, *, tc, valid):
    c = pl.program_id(1)
    nl = int(math.log2(tc))
    hdk = GLA_HEADS * GLA_DK

    @pl.when(c == 0)
    def _():
        st_ref[...] = s0_ref[0]

    q = q_ref[0] * (GLA_DK ** -0.5)
    k = k_ref[0]
    v = v_ref[0]
    x = _dot(r_ref[0].astype(BF16), wgk_ref[...]) + bgk_ref[...]
    gk = _log_sigmoid_pair(x)[0] / GLA_GATE_NORM
    if valid < tc:
        live = lax.broadcasted_iota(jnp.int32, gk.shape, 0) < valid
        gk = jnp.where(live, gk, 0.0)
        k = jnp.where(live, k, 0.0)
    g3 = jnp.concatenate(_split3(gk), axis=1)
    r3 = _dot(dall_ref[...], g3)
    ex = jnp.exp(r3[:, :hdk] + r3[:, hdk:2 * hdk] + r3[:, 2 * hdk:])
    e_cum = ex[nl * tc:(nl + 1) * tc]
    e_rem = ex[(nl + 1) * tc:]
    bl = _dot_tn(g3, jnp.ones((tc, GLA_DV), BF16))
    decay = jnp.exp(bl[:hdk] + bl[hdk:2 * hdk] + bl[2 * hdk:])

    row = lax.broadcasted_iota(jnp.int32, q.shape, 0)
    lane = lax.broadcasted_iota(jnp.int32, (tc, 2 * GLA_DK), 1)
    lvl = lvl_ref[...]
    q_in = (q * e_cum).astype(BF16)
    k_out = (k * e_rem).astype(BF16)
    xs = []
    for j in range(nl):
        is_q = ((row >> j) & 1) == 1
        xs.append((jnp.where(is_q, q, k) * ex[j * tc:(j + 1) * tc]).astype(BF16))
    qb, kb = q.astype(BF16), k.astype(BF16)
    gn = gn_ref[...]
    heads = range(GLA_HEADS)
    ps = [slice((h // 2) * 2 * GLA_DK, (h // 2 + 1) * 2 * GLA_DK) for h in heads]
    in_head = [(lane < GLA_DK) if h % 2 == 0 else (lane >= GLA_DK) for h in heads]
    g_diag = [_dot_nt(jnp.where(in_head[h], qb[:, ps[h]], 0), kb[:, ps[h]]) for h in heads]
    g_lvl = [[_dot_nt(jnp.where(in_head[h], xs[j][:, ps[h]], 0), xs[j][:, ps[h]]) for h in heads] for j in range(nl)]
    vh = [v[:, h * GLA_DV:(h + 1) * GLA_DV].astype(BF16) for h in heads]
    st = [st_ref[pair] for pair in range(GLA_HEADS // 2)]
    o_inter = [_dot(jnp.where(in_head[h], q_in[:, ps[h]], 0), st[h // 2].astype(BF16)) for h in heads]
    upd = [_dot_tn(k_out[:, ps[h]], vh[h]) for h in heads]
    att = []
    for h in heads:
        a = jnp.where(lvl == -1, g_diag[h], 0.0)
        for j in range(nl):
            a = jnp.where(lvl == j, g_lvl[j][h], a)
        att.append(a.astype(BF16))
    o_intra = [_dot(att[h], vh[h]) for h in heads]
    rsel = lax.broadcasted_iota(jnp.int32, (2 * GLA_DK, GLA_DV), 0)
    for pair in range(GLA_HEADS // 2):
        new = jnp.where(rsel < GLA_DK, upd[2 * pair], upd[2 * pair + 1])
        st_ref[pair] = decay[ps[2 * pair]] * st[pair] + new
    for h in heads:
        o = o_intra[h] + o_inter[h]
        o = o * lax.rsqrt(jnp.mean(o * o, axis=-1, keepdims=True) + EPS) * gn
        gh = g_ref[0, :, h * GLA_DV:(h + 1) * GLA_DV]
        o_ref[0, :, h * GLA_DV:(h + 1) * GLA_DV] = o * (gh * jax.nn.sigmoid(gh))

    sfin_ref[0] = st_ref[...]


def _gla(p3, w_gk, b_gk, gla_norm, s0, tc, valid):
    b, s, _ = p3.shape
    hdk, hdv = GLA_HEADS * GLA_DK, GLA_HEADS * GLA_DV
    u_w = 2 * hdk
    dall, lvl = _gla_constants(tc)
    const = lambda shape: pl.BlockSpec(shape, lambda i, j: tuple(0 for _ in shape))
    return pl.pallas_call(
        functools.partial(_gla_kernel, tc=tc, valid=valid),
        grid=(b, s // tc),
        in_specs=[pl.BlockSpec((1, tc, hdk), lambda i, j: (i, j, u_w // hdk)),
                  pl.BlockSpec((1, tc, hdk), lambda i, j: (i, j, u_w // hdk + 1)),
                  pl.BlockSpec((1, tc, hdv), lambda i, j: (i, j, (u_w + 2 * hdk) // hdv)),
                  pl.BlockSpec((1, tc, hdv), lambda i, j: (i, j, (u_w + 2 * hdk) // hdv + 1)),
                  pl.BlockSpec((1, tc, LANES), lambda i, j: (i, j, (u_w + 2 * hdk + 2 * hdv) // LANES)),
                  const((LANES, hdk)), const((1, hdk)), const((1, GLA_DV)),
                  pl.BlockSpec((1, GLA_HEADS // 2, 2 * GLA_DK, GLA_DV), lambda i, j: (i, 0, 0, 0)),
                  const(dall.shape), const(lvl.shape)],
        out_specs=[pl.BlockSpec((1, tc, hdv), lambda i, j: (i, j, 0)),
                   pl.BlockSpec((1, GLA_HEADS // 2, 2 * GLA_DK, GLA_DV), lambda i, j: (i, 0, 0, 0))],
        out_shape=[jax.ShapeDtypeStruct((b, s, hdv), F32),
                   jax.ShapeDtypeStruct((b, GLA_HEADS // 2, 2 * GLA_DK, GLA_DV), F32)],
        scratch_shapes=[pltpu.VMEM((GLA_HEADS // 2, 2 * GLA_DK, GLA_DV), F32)],
        compiler_params=_cparams(("parallel", "arbitrary")),
        name="gla",
    )(p3, p3, p3, p3, p3, w_gk, b_gk, gla_norm, s0, dall, lvl)


def _mix_out_kernel(x_ref, yc_ref, u_ref, og_ref, d_ref, wglu_ref, bglu_ref, wtop_ref, wbot_ref, o_ref):
    y = yc_ref[...] + d_ref[...] * u_ref[...]
    z = jax.nn.gelu(y)
    s5 = z * jax.nn.sigmoid(_dot(z.astype(BF16), wglu_ref[...]) + bglu_ref[...])
    o_ref[...] = (x_ref[...] + _dot(s5.astype(BF16), wtop_ref[...])
                  + _dot(og_ref[...].astype(BF16), wbot_ref[...]))


def _mix_out(x, y_conv, p, o_gla, d_skip, w_glu, b_glu, w_top, w_bot):
    t, d = x.shape
    w5 = y_conv.shape[1]
    tm = _tile(t, 512)
    row = lambda w: pl.BlockSpec((tm, w), lambda i: (i, 0))
    const = lambda a, b: pl.BlockSpec((a, b), lambda i: (0, 0))
    return pl.pallas_call(
        _mix_out_kernel,
        grid=(t // tm,),
        in_specs=[row(d), row(w5), row(w5), row(o_gla.shape[1]), const(1, w5), const(w5, w5), const(1, w5),
                  const(w5, d), const(o_gla.shape[1], d)],
        out_specs=row(d),
        out_shape=jax.ShapeDtypeStruct((t, d), F32),
        compiler_params=_cparams(("parallel",)),
        name="mix_out",
    )(x, y_conv, p, o_gla, d_skip, w_glu, b_glu, w_top, w_bot)


def _ffn_kernel(x_ref, gn_ref, hist_ref, wu_ref, wg_ref, wc_ref, bc_ref, wd_ref, gf_ref,
                o_ref, hist_out_ref, h_scr, acc_scr, pad_scr, carry_scr, *, rs, hist_rows, final_norm):
    t = pl.program_id(1)
    c = pl.program_id(2)
    tm = x_ref.shape[1]

    @pl.when(c == 0)
    def _():
        h_scr[...] = _rmsnorm(x_ref[0], gn_ref[...]).astype(BF16)
        acc_scr[...] = jnp.zeros_like(acc_scr)

    @pl.when(t == 0)
    def _():
        carry_scr[c] = hist_ref[0]

    h = h_scr[...]
    u = _dot(h, wu_ref[...])
    g = _dot(h, wg_ref[...])
    pad_scr[0:hist_rows] = carry_scr[c]
    pad_scr[hist_rows:] = u
    wc = wc_ref[...]
    uc = bc_ref[...]
    for j in range(CONV_W):
        off = hist_rows - (CONV_W - 1 - j) * rs
        uc = uc + wc[j:j + 1] * pad_scr[off:off + tm]
    new_hist = pad_scr[tm:tm + hist_rows]
    carry_scr[c] = new_hist
    hist_out_ref[0, 0] = new_hist
    hdn = jax.nn.gelu(uc) * g
    acc_scr[...] += _dot(hdn.astype(BF16), wd_ref[...])

    @pl.when(c == pl.num_programs(2) - 1)
    def _():
        y = x_ref[0] + acc_scr[...]
        if final_norm:
            y = _rmsnorm(y, gf_ref[...])
        o_ref[0] = y


def _ffn(x3, g_norm, hist, w_up, w_conv, b_conv, w_down, g_final, rs, final_norm):
    nb, s, d = x3.shape
    dff = w_down.shape[0]
    hist_rows = hist.shape[1]
    tm = _tile(s, 512)
    nt = s // tm
    ck = dff // 2 if (dff // 2) % LANES == 0 else dff
    nck = dff // ck
    y, hist_all = pl.pallas_call(
        functools.partial(_ffn_kernel, rs=rs, hist_rows=hist_rows, final_norm=final_norm),
        grid=(nb, s // tm, nck),
        in_specs=[pl.BlockSpec((1, tm, d), lambda b, t, c: (b, t, 0)),
                  pl.BlockSpec((1, d), lambda b, t, c: (0, 0)),
                  pl.BlockSpec((1, hist_rows, ck), lambda b, t, c: (b, 0, c)),
                  pl.BlockSpec((d, ck), lambda b, t, c: (0, c)),
                  pl.BlockSpec((d, ck), lambda b, t, c: (0, nck + c)),
                  pl.BlockSpec((CONV_W, ck), lambda b, t, c: (0, c)),
                  pl.BlockSpec((1, ck), lambda b, t, c: (0, c)),
                  pl.BlockSpec((ck, d), lambda b, t, c: (c, 0)),
                  pl.BlockSpec((1, d), lambda b, t, c: (0, 0))],
        out_specs=[pl.BlockSpec((1, tm, d), lambda b, t, c: (b, t, 0)),
                   pl.BlockSpec((1, 1, hist_rows, ck), lambda b, t, c: (b, t, 0, c))],
        out_shape=[jax.ShapeDtypeStruct((nb, s, d), F32), jax.ShapeDtypeStruct((nb, nt, hist_rows, dff), F32)],
        scratch_shapes=[pltpu.VMEM((tm, d), BF16), pltpu.VMEM((tm, d), F32),
                        pltpu.VMEM((hist_rows + tm, ck), F32), pltpu.VMEM((nck, hist_rows, ck), F32)],
        compiler_params=_cparams(("parallel", "arbitrary", "arbitrary")),
        name="ffn",
    )(x3, g_norm, hist, w_up, w_up, w_conv, b_conv, w_down, g_final)
    return y, hist_all[:, nt - 1]


def _sb_chains(zs, cbs, mo, mask):
    lsps, lons, firsts = [], [], []
    for chain in zs:
        for z in chain:
            lon = jnp.maximum(z, 0.0) + jnp.log2(1.0 + jnp.exp2(-jnp.abs(z)))
            lsps.append(z - lon)
            lon = lon if mask is None else jnp.where(mask, lon, 0.0)
            firsts.append(lon[:, :1])
            lons.append(lon.astype(BF16))
    rs = [_dot(lon, mo) for lon in lons]
    out, new_cbs, n = [], [], 0
    for chain, cb in zip(zs, cbs):
        weights = []
        for z in chain:
            tk = z.shape[1]
            a = jnp.exp2(lsps[n] + rs[n] + jnp.tile(cb, (1, tk // LANES)))
            weights.append((a if mask is None else jnp.where(mask, a, 0.0)).astype(BF16))
            cb = cb + jnp.broadcast_to(rs[n][:, :1] - firsts[n], cb.shape)
            n += 1
        out.append(weights)
        new_cbs.append(cb)
    return out, new_cbs


SB_TILES_PER_ITER = 4


def _sb_mo(tk):
    j = np.arange(tk)[:, None]
    s = np.arange(tk)[None, :]
    return jnp.asarray(-(j > s).astype(np.float32), BF16)


def _sb_prompt_kernel(bias_ref, q_ref, kt_ref, vt_ref, mo_ref, o_ref, acc_ref, cb_ref, *, tq):
    hp = pl.program_id(1)
    qi = pl.program_id(2)
    q = q_ref[0]
    lane = lax.broadcasted_iota(jnp.int32, q.shape, 1)
    tri = lax.broadcasted_iota(jnp.int32, (tq, tq), 1) < lax.broadcasted_iota(jnp.int32, (tq, tq), 0)
    mo = mo_ref[...]
    qh = [jnp.where(lane < SB_DIM, q, 0), jnp.where(lane >= SB_DIM, q, 0)]
    bias = [bias_ref[2 * hp], bias_ref[2 * hp + 1]]

    def tiles(kbs, mask, first):
        kts = [kt_ref[0, kb] for kb in kbs]
        zs = [[_dot(qh[sub], kt) + bias[sub] for kt in kts] for sub in range(2)]
        cbs = [jnp.zeros((tq, LANES), F32) if first else cb_ref[sub] for sub in range(2)]
        weights, cbs = _sb_chains(zs, cbs, mo, mask)
        vts = [vt_ref[0, kb] for kb in kbs]
        pvs = [[_dot_nt(a, vt) for a, vt in zip(weights[sub], vts)] for sub in range(2)]
        for sub in range(2):
            acc = sum(pvs[sub][1:], pvs[sub][0])
            acc_ref[sub] = acc if first else acc_ref[sub] + acc
            cb_ref[sub] = cbs[sub]

    tiles([qi], tri, True)

    def body(i, carry):
        kb = qi - 1 - SB_TILES_PER_ITER * i
        tiles([kb - n for n in range(SB_TILES_PER_ITER)], None, False)
        return carry

    lax.fori_loop(0, qi // SB_TILES_PER_ITER, body, 0)
    rem = qi % SB_TILES_PER_ITER
    n = SB_TILES_PER_ITER // 2
    while n:
        @pl.when((rem & n) != 0)
        def _(n=n):
            top = (rem & (2 * n - 1)) - 1
            tiles([top - m for m in range(n)], None, False)
        n //= 2

    o_ref[0] = jnp.where(lane < SB_DIM, acc_ref[0], acc_ref[1])


def _sb_prompt(q_b, kt_b, vt_b, bias2):
    b, s, hd = q_b.shape
    tq = kt_b.shape[3]
    nt = s // tq
    return pl.pallas_call(
        functools.partial(_sb_prompt_kernel, tq=tq),
        grid=(b, hd // LANES, nt),
        in_specs=[pl.BlockSpec(memory_space=pltpu.SMEM),
                  pl.BlockSpec((1, tq, LANES), lambda i, h, j: (i, j, h)),
                  pl.BlockSpec((1, nt, LANES, tq), lambda i, h, j: (i, 0, h, 0)),
                  pl.BlockSpec((1, nt, LANES, tq), lambda i, h, j: (i, 0, h, 0)),
                  pl.BlockSpec((tq, tq), lambda i, h, j: (0, 0))],
        out_specs=pl.BlockSpec((1, tq, LANES), lambda i, h, j: (i, j, h)),
        out_shape=jax.ShapeDtypeStruct((b, s, hd), F32),
        scratch_shapes=[pltpu.VMEM((2, tq, LANES), F32), pltpu.VMEM((2, tq, LANES), F32)],
        compiler_params=_cparams(("parallel", "parallel", "arbitrary")),
        name="sb_prompt",
    )(bias2, q_b, kt_b, vt_b, _sb_mo(tq))


SB_PAGES_PER_STEP = 4


def _sb_sample_kernel(pt_ref, qbd_ref, bias_ref, kn_ref, vn_ref, *rest, nq, npp):
    ck_refs, cv_refs = rest[:npp], rest[npp:2 * npp]
    mo_ref, o_ref, acc_ref, cb_ref = rest[2 * npp:]
    j = pl.program_id(1)
    qbd = qbd_ref[0]
    bias = bias_ref[...]
    mo = mo_ref[...]
    nh = SB_HEADS

    @pl.when(j == 0)
    def _():
        z = _dot_nt(qbd, kn_ref[0].astype(BF16)) + bias
        s_idx = lax.broadcasted_iota(jnp.int32, z.shape, 1)
        q_idx = lax.broadcasted_iota(jnp.int32, z.shape, 0) // nh
        w, cbs = _sb_chains([[z]], [jnp.zeros((z.shape[0], LANES), F32)], mo, s_idx < q_idx)
        acc_ref[...] = _dot(w[0][0], vn_ref[0].astype(BF16))
        cb_ref[...] = cbs[0]

    zs = [_dot(qbd, ck_refs[i][0, 0].astype(BF16)) + bias for i in range(npp)]
    w, cbs = _sb_chains([zs], [cb_ref[...]], mo, None)
    pvs = [_dot_nt(w[0][i], cv_refs[i][0, 0].astype(BF16)) for i in range(npp)]
    acc = acc_ref[...] + sum(pvs[1:], pvs[0])
    acc_ref[...] = acc
    cb_ref[...] = cbs[0]

    @pl.when(j == pl.num_programs(1) - 1)
    def _():
        r_h = lax.broadcasted_iota(jnp.int32, acc.shape, 0) % nh
        l_h = lax.broadcasted_iota(jnp.int32, acc.shape, 1) // SB_DIM
        sel = jnp.where(r_h == l_h, acc, 0.0)
        o_ref[0] = jnp.sum(sel.reshape(nq, nh, acc.shape[1]), axis=1)


def _sb_sample(q_b, k_new, v_new, cache_kt, cache_vt, layer, page_table, bias2):
    b, nq, hd = q_b.shape
    nh = SB_HEADS
    n_pages = page_table.shape[1]
    page = cache_kt.shape[3]
    npp = SB_PAGES_PER_STEP if n_pages % SB_PAGES_PER_STEP == 0 else 1
    qs = q_b.reshape(b, nq, nh, SB_DIM)
    eye = jnp.eye(nh, dtype=BF16)
    qbd = (qs[:, :, :, None, :] * eye[None, None, :, :, None]).reshape(b, nq * nh, hd)
    bias_b = jnp.broadcast_to(jnp.tile(bias2, nq)[:, None], (nq * nh, page))
    pad = lambda a: jnp.pad(a, ((0, 0), (0, page - nq), (0, 0)))
    page_spec = lambda i_page: pl.BlockSpec(
        (1, 1, hd, page), lambda i, j, pt: (layer, pt[i, n_pages - 1 - (j * npp + i_page)], 0, 0))
    return pl.pallas_call(
        functools.partial(_sb_sample_kernel, nq=nq, npp=npp),
        grid_spec=pltpu.PrefetchScalarGridSpec(
            num_scalar_prefetch=1,
            grid=(b, n_pages // npp),
            in_specs=([pl.BlockSpec((1, nq * nh, hd), lambda i, j, pt: (i, 0, 0)),
                       pl.BlockSpec((nq * nh, page), lambda i, j, pt: (0, 0)),
                       pl.BlockSpec((1, page, hd), lambda i, j, pt: (i, 0, 0)),
                       pl.BlockSpec((1, page, hd), lambda i, j, pt: (i, 0, 0))]
                      + [page_spec(i) for i in range(npp)] + [page_spec(i) for i in range(npp)]
                      + [pl.BlockSpec((page, page), lambda i, j, pt: (0, 0))]),
            out_specs=pl.BlockSpec((1, nq, hd), lambda i, j, pt: (i, 0, 0)),
            scratch_shapes=[pltpu.VMEM((nq * nh, hd), F32), pltpu.VMEM((nq * nh, LANES), F32)],
        ),
        out_shape=jax.ShapeDtypeStruct((b, nq, hd), F32),
        compiler_params=_cparams(("parallel", "arbitrary")),
        name="sb_sample",
    )(page_table, qbd, bias_b, pad(k_new), pad(v_new), *([cache_kt] * npp), *([cache_vt] * npp), _sb_mo(page))


def _even_layer(x3, g_mix, w_in_p, s5_ops, d_skip, w_glu, b_glu, w_gk_p, b_gk, gla_norm, w_top, w_bot,
                s5_h0, gla_s0, chunk):
    b, s, d = x3.shape
    s5_w = d_skip.shape[1]
    ng = s5_w // S5_GROUP
    nc = s // chunk
    x2 = x3.reshape(b * s, d)
    p = _inproj(x2, g_mix, w_in_p)
    u_g = (p[:, :s5_w].reshape(b, nc, chunk, ng, S5_GROUP).transpose(3, 0, 1, 2, 4)
           .reshape(ng, b * nc, chunk * S5_GROUP))
    if s5_h0 is None:
        h0r = h0i = None
    else:
        h0r, h0i = s5_h0[0].transpose(1, 0, 2), s5_h0[1].transpose(1, 0, 2)
    y_g, hre, him = _s5(u_g, s5_ops, h0r, h0i, nc)
    y_conv = (y_g.reshape(ng, b, nc, chunk, S5_GROUP).transpose(1, 2, 3, 0, 4).reshape(b * s, s5_w))
    p_state = hre.shape[2]
    fin = lambda a: a.reshape(ng, b, nc, p_state)[:, :, -1].transpose(1, 0, 2)
    p3 = p.reshape(b, s, p.shape[1])
    gla_tc = _tile(s, 128) if s >= 16 else 16
    gla_valid = min(s, gla_tc)
    sp = p3 if s >= gla_tc else jnp.pad(p3, ((0, 0), (0, gla_tc - s), (0, 0)))
    s0 = (jnp.zeros((b, GLA_HEADS // 2, 2 * GLA_DK, GLA_DV), F32) if gla_s0 is None
          else gla_s0.reshape(b, GLA_HEADS // 2, 2 * GLA_DK, GLA_DV))
    o_gla, s_fin = _gla(sp, w_gk_p, b_gk, gla_norm, s0, gla_tc, gla_valid)
    o_gla = o_gla[:, :s].reshape(b * s, o_gla.shape[2])
    x2 = _mix_out(x2, y_conv, p, o_gla, d_skip, w_glu, b_glu, w_top, w_bot)
    return x2.reshape(b, s, d), fin(hre), fin(him), s_fin.reshape(b, GLA_HEADS, GLA_DK, GLA_DV)


def kernel(x_prompt, x_sample, state_s5_re, state_s5_im, state_gla, cache_k, cache_v, state_ffn_conv, page_table, norm_mix, norm_ffn, norm_final, w_in0, s5_a_re, s5_a_im, s5_log_dt, s5_b_re, s5_b_im, s5_c_re, s5_c_im, s5_d, s5_w_glu, s5_b_glu, gla_w_gk, gla_b_gk, gla_norm, w_out0, w_qkv1, w_out1, sb_bias, ffn_w_up, ffn_w_conv, ffn_b_conv, ffn_w_down):
    bp, sp, d = x_prompt.shape
    bs, ss, _ = x_sample.shape
    depth = norm_mix.shape[0]
    dff = ffn_w_down.shape[1]
    s5_w = s5_d.shape[1]
    xp, xs = x_prompt, x_sample
    outs_p = {k: [] for k in ("s5r", "s5i", "gla", "k", "v", "conv")}
    outs_s = {k: [] for k in ("s5r", "s5i", "gla", "k", "v", "conv")}
    hist_rows_p = 8
    for l in range(depth):
        i = l // 2
        g_mix = norm_mix[l][None, :]
        if l % 2 == 0:
            in0 = w_in0.shape[2]
            in0p = -(-in0 // LANES) * LANES
            w_in_p = jnp.pad(w_in0[i], ((0, 0), (0, in0p - in0))).astype(BF16)
            w_gk_p = jnp.pad(gla_w_gk[i], ((0, LANES - GLA_RANK), (0, 0))).astype(BF16)
            common = (s5_d[i][None, :], s5_w_glu[i].astype(BF16), s5_b_glu[i][None, :], w_gk_p,
                      gla_b_gk[i][None, :], gla_norm[i][None, :], w_out0[i][:s5_w].astype(BF16),
                      w_out0[i][s5_w:].astype(BF16))
            s5_args = (s5_a_re[i], s5_a_im[i], s5_log_dt[i], s5_b_re[i], s5_b_im[i], s5_c_re[i], s5_c_im[i])
            chunk_p = _tile(sp, S5_CHUNK)
            ops_p = _s5_operators(*s5_args, chunk_p, sp // chunk_p)
            ops_s = _s5_operators(*s5_args, ss, 1)
            xp, hr, hi, sg = _even_layer(xp, g_mix, w_in_p, ops_p, *common, None, None, chunk_p)
            xs, hr2, hi2, sg2 = _even_layer(xs, g_mix, w_in_p, ops_s, *common,
                                            (state_s5_re[i], state_s5_im[i]), state_gla[i], ss)
            for o, vals in ((outs_p, (hr, hi, sg)), (outs_s, (hr2, hi2, sg2))):
                o["s5r"].append(vals[0])
                o["s5i"].append(vals[1])
                o["gla"].append(vals[2])
        else:
            w_qkv = w_qkv1[i].astype(BF16)
            w_out = w_out1[i].astype(BF16)
            bias2 = sb_bias[i] * LOG2E
            qb, kt, vt, ktb, vtb = _qkv_t(xp, g_mix, w_qkv[:, :d], w_qkv[:, d:].T, _tile(sp, 256))
            att = _sb_prompt(qb, ktb, vtb, bias2)
            xp = _proj_res(xp.reshape(bp * sp, d), att.reshape(bp * sp, d), w_out).reshape(bp, sp, d)
            qsb, ks, vs, _, _ = _qkv(xs.reshape(bs * ss, d), g_mix, w_qkv)
            as_t = lambda c: c.transpose(0, 1, 3, 4, 2).reshape(c.shape[0], c.shape[1], d, c.shape[2])
            att_s = _sb_sample(qsb.reshape(bs, ss, d), ks.reshape(bs, ss, d), vs.reshape(bs, ss, d),
                               as_t(cache_k), as_t(cache_v), i, page_table, bias2)
            xs = _proj_res(xs.reshape(bs * ss, d), att_s.reshape(bs * ss, d), w_out).reshape(bs, ss, d)
            outs_p["k"].append(kt.reshape(bp, SB_HEADS, SB_DIM, sp).transpose(0, 3, 1, 2))
            outs_p["v"].append(vt.reshape(bp, SB_HEADS, SB_DIM, sp).transpose(0, 3, 1, 2))
            outs_s["k"].append(ks.reshape(bs, ss, SB_HEADS, SB_DIM))
            outs_s["v"].append(vs.reshape(bs, ss, SB_HEADS, SB_DIM))
        final = l == depth - 1
        w_up = ffn_w_up[l].astype(BF16)
        w_down = ffn_w_down[l].astype(BF16)
        ffn_args = (w_up, ffn_w_conv[l], ffn_b_conv[l][None, :], w_down, norm_final[None, :])
        xp, hist_p = _ffn(xp, norm_ffn[l][None, :], jnp.zeros((bp, hist_rows_p, dff), F32), *ffn_args, 1, final)
        outs_p["conv"].append(hist_p[:, hist_rows_p - (CONV_W - 1):])
        hist_s = state_ffn_conv[l].transpose(1, 0, 2).reshape(1, (CONV_W - 1) * bs, dff)
        xs_t, hist_s = _ffn(xs.transpose(1, 0, 2).reshape(1, ss * bs, d), norm_ffn[l][None, :], hist_s,
                            *ffn_args, bs, final)
        xs = xs_t.reshape(ss, bs, d).transpose(1, 0, 2)
        outs_s["conv"].append(hist_s.reshape(CONV_W - 1, bs, dff).transpose(1, 0, 2))
    st = jnp.stack
    return (xp, xs,
            st(outs_p["s5r"]), st(outs_p["s5i"]), st(outs_p["gla"]), st(outs_p["k"]), st(outs_p["v"]), st(outs_p["conv"]),
            st(outs_s["s5r"]), st(outs_s["s5i"]), st(outs_s["gla"]), st(outs_s["k"]), st(outs_s["v"]), st(outs_s["conv"]))
```

```python
import functools
import math

import jax
import jax.numpy as jnp
import numpy as np
from jax import lax
from jax.experimental import pallas as pl
from jax.experimental.pallas import tpu as pltpu

F32 = jnp.float32
BF16 = jnp.bfloat16
EPS = 1e-6

S5_GROUP = 16
S5_CHUNK = 16
GLA_HEADS = 4
GLA_DK = 64
GLA_DV = 128
GLA_RANK = 16
GLA_GATE_NORM = 16.0
SB_HEADS = 16
SB_DIM = 64
CONV_W = 3
LOG2E = math.log2(math.e)
SB_QSCALE = SB_DIM ** -0.5 * LOG2E
LANES = 128
VMEM_LIMIT = 56 * 1024 * 1024


def _cparams(sem):
    return pltpu.CompilerParams(dimension_semantics=sem, vmem_limit_bytes=VMEM_LIMIT)


def _tile(n, pref):
    if n <= pref:
        return n
    t = pref
    while n % t:
        t //= 2
    return t


def _rmsnorm(x, g):
    y = x * lax.rsqrt(jnp.mean(x * x, axis=-1, keepdims=True) + EPS)
    return y * g


def _split2(x):
    hi = x.astype(BF16)
    lo = (x - hi.astype(F32)).astype(BF16)
    return hi, lo


def _split3(x):
    hi = x.astype(BF16)
    r = x - hi.astype(F32)
    mid = r.astype(BF16)
    lo = (r - mid.astype(F32)).astype(BF16)
    return hi, mid, lo


def _dot(a, b):
    return jnp.dot(a, b, preferred_element_type=F32)


def _dot_nt(a, b):
    return lax.dot_general(a, b, (((1,), (1,)), ((), ())), preferred_element_type=F32)


def _dot_tn(a, b):
    return lax.dot_general(a, b, (((0,), (0,)), ((), ())), preferred_element_type=F32)


def _log_sigmoid_pair(z):
    lsp = jnp.minimum(z, 0.0) - jnp.log1p(jnp.exp(-jnp.abs(z)))
    return lsp, lsp - z


def _inproj_kernel(x_ref, g_ref, w_ref, o_ref):
    h = _rmsnorm(x_ref[...], g_ref[...]).astype(BF16)
    o_ref[...] = _dot(h, w_ref[...])


def _inproj(x, g, w):
    t, d = x.shape
    n = w.shape[1]
    tm = _tile(t, 512)
    return pl.pallas_call(
        _inproj_kernel,
        grid=(t // tm,),
        in_specs=[pl.BlockSpec((tm, d), lambda i: (i, 0)),
                  pl.BlockSpec((1, d), lambda i: (0, 0)),
                  pl.BlockSpec((d, n), lambda i: (0, 0))],
        out_specs=pl.BlockSpec((tm, n), lambda i: (i, 0)),
        out_shape=jax.ShapeDtypeStruct((t, n), F32),
        compiler_params=_cparams(("parallel",)),
        name="inproj",
    )(x, g, w)


def _qkv_kernel(x_ref, g_ref, w_ref, qb_ref, k_ref, v_ref, kb_ref, vb_ref):
    d = x_ref.shape[1]
    h = _rmsnorm(x_ref[...], g_ref[...]).astype(BF16)
    p = _dot(h, w_ref[...])
    q, k, v = p[:, :d], p[:, d:2 * d], p[:, 2 * d:]
    qb_ref[...] = (q * SB_QSCALE).astype(BF16)
    k_ref[...] = k
    v_ref[...] = v
    kb_ref[...] = k.astype(BF16)
    vb_ref[...] = v.astype(BF16)


def _qkv(x, g, w):
    t, d = x.shape
    tm = _tile(t, 256)
    row = pl.BlockSpec((tm, d), lambda i: (i, 0))
    return pl.pallas_call(
        _qkv_kernel,
        grid=(t // tm,),
        in_specs=[row, pl.BlockSpec((1, d), lambda i: (0, 0)), pl.BlockSpec((d, 3 * d), lambda i: (0, 0))],
        out_specs=[row, row, row, row, row],
        out_shape=[jax.ShapeDtypeStruct((t, d), BF16), jax.ShapeDtypeStruct((t, d), F32),
                   jax.ShapeDtypeStruct((t, d), F32), jax.ShapeDtypeStruct((t, d), BF16),
                   jax.ShapeDtypeStruct((t, d), BF16)],
        compiler_params=_cparams(("parallel",)),
        name="qkv",
    )(x, g, w)


def _qkv_t_kernel(x_ref, g_ref, wq_ref, wkvt_ref, qb_ref, kt_ref, vt_ref, ktb_ref, vtb_ref):
    d = x_ref.shape[2]
    h = _rmsnorm(x_ref[0], g_ref[...]).astype(BF16)
    qb_ref[0] = (_dot(h, wq_ref[...]) * SB_QSCALE).astype(BF16)
    kvt = _dot_nt(wkvt_ref[...], h)
    kt_ref[0] = kvt[:d]
    vt_ref[0] = kvt[d:]
    ktb_ref[0, 0] = kvt[:d].astype(BF16)
    vtb_ref[0, 0] = kvt[d:].astype(BF16)


def _qkv_t(x3, g, wq, wkvt, tm):
    b, s, d = x3.shape
    nt = s // tm
    return pl.pallas_call(
        _qkv_t_kernel,
        grid=(b, nt),
        in_specs=[pl.BlockSpec((1, tm, d), lambda i, t: (i, t, 0)),
                  pl.BlockSpec((1, d), lambda i, t: (0, 0)),
                  pl.BlockSpec((d, d), lambda i, t: (0, 0)),
                  pl.BlockSpec((2 * d, d), lambda i, t: (0, 0))],
        out_specs=[pl.BlockSpec((1, tm, d), lambda i, t: (i, t, 0)),
                   pl.BlockSpec((1, d, tm), lambda i, t: (i, 0, t)),
                   pl.BlockSpec((1, d, tm), lambda i, t: (i, 0, t)),
                   pl.BlockSpec((1, 1, d, tm), lambda i, t: (i, t, 0, 0)),
                   pl.BlockSpec((1, 1, d, tm), lambda i, t: (i, t, 0, 0))],
        out_shape=[jax.ShapeDtypeStruct((b, s, d), BF16), jax.ShapeDtypeStruct((b, d, s), F32),
                   jax.ShapeDtypeStruct((b, d, s), F32), jax.ShapeDtypeStruct((b, nt, d, tm), BF16),
                   jax.ShapeDtypeStruct((b, nt, d, tm), BF16)],
        compiler_params=_cparams(("parallel", "parallel")),
        name="qkv_t",
    )(x3, g, wq, wkvt)


def _proj_res_kernel(x_ref, a_ref, w_ref, o_ref):
    o_ref[...] = x_ref[...] + _dot(a_ref[...].astype(BF16), w_ref[...])


def _proj_res(x, a, w):
    t, d = x.shape
    k = a.shape[1]
    tm = _tile(t, 512)
    return pl.pallas_call(
        _proj_res_kernel,
        grid=(t // tm,),
        in_specs=[pl.BlockSpec((tm, d), lambda i: (i, 0)),
                  pl.BlockSpec((tm, k), lambda i: (i, 0)),
                  pl.BlockSpec((k, d), lambda i: (0, 0))],
        out_specs=pl.BlockSpec((tm, d), lambda i: (i, 0)),
        out_shape=jax.ShapeDtypeStruct((t, d), F32),
        compiler_params=_cparams(("parallel",)),
        name="proj_res",
    )(x, a, w)


S5_BLOCK_GROUPS = LANES // S5_GROUP


def _s5_operators(a_re, a_im, log_dt, b_re, b_im, c_re, c_im, chunk, n_chunks):
    hp = lax.Precision.HIGHEST
    g, p = a_re.shape
    dt = jnp.exp(log_dt)[:, None]
    mag = jnp.exp(a_re * dt)
    ab_re, ab_im = mag * jnp.cos(a_im * dt), mag * jnp.sin(a_im * dt)
    den = a_re * a_re + a_im * a_im
    nr, ni = ab_re - 1.0, ab_im
    coef_re = (nr * a_re + ni * a_im) / den
    coef_im = (ni * a_re - nr * a_im) / den
    bb_re = coef_re[..., None] * b_re - coef_im[..., None] * b_im
    bb_im = coef_re[..., None] * b_im + coef_im[..., None] * b_re
    pr, pi = [jnp.ones_like(ab_re)], [jnp.zeros_like(ab_re)]
    for _ in range(chunk):
        r0, i0 = pr[-1], pi[-1]
        pr.append(r0 * ab_re - i0 * ab_im)
        pi.append(r0 * ab_im + i0 * ab_re)
    pw_re, pw_im = jnp.stack(pr, 1), jnp.stack(pi, 1)
    ca_re = c_re[:, None] * pw_re[:, :, None, :] - c_im[:, None] * pw_im[:, :, None, :]
    ca_im = c_re[:, None] * pw_im[:, :, None, :] + c_im[:, None] * pw_re[:, :, None, :]
    taps = (jnp.einsum('gtcp,gpd->gtcd', ca_re[:, :chunk], bb_re, precision=hp)
            - jnp.einsum('gtcp,gpd->gtcd', ca_im[:, :chunk], bb_im, precision=hp))
    gb = S5_BLOCK_GROUPS
    nb = g // gb
    lb = chunk * LANES

    def block_diag(x, perm, g_axis, new_axis, shape):
        x = jnp.expand_dims(lax.optimization_barrier(x.transpose(perm)), new_axis)
        same = (lax.broadcasted_iota(jnp.int32, x.shape[:new_axis] + (gb,) + x.shape[new_axis + 1:], g_axis)
                == lax.broadcasted_iota(jnp.int32, x.shape[:new_axis] + (gb,) + x.shape[new_axis + 1:], new_axis))
        return lax.optimization_barrier(jnp.where(same, x, 0.0).astype(BF16)).reshape(shape)

    s_idx = np.arange(chunk)[:, None]
    i_idx = np.arange(chunk)[None, :]
    tau = np.clip(i_idx - s_idx, 0, chunk - 1)
    m = jnp.where((i_idx >= s_idx)[None, :, :, None, None], taps[:, tau], 0.0)
    m = block_diag(m.reshape(nb, gb, chunk, chunk, S5_GROUP, S5_GROUP), (0, 2, 1, 5, 3, 4), 2, 5, (nb, lb, lb))
    rev_re, rev_im = jnp.stack(pr[chunk - 1::-1], 1), jnp.stack(pi[chunk - 1::-1], 1)
    bt_re, bt_im = bb_re.transpose(0, 2, 1)[:, None], bb_im.transpose(0, 2, 1)[:, None]
    w_re = rev_re[:, :, None, :] * bt_re - rev_im[:, :, None, :] * bt_im
    w_im = rev_re[:, :, None, :] * bt_im + rev_im[:, :, None, :] * bt_re
    blockw = lambda w: block_diag(w.reshape(nb, gb, chunk, S5_GROUP, p), (0, 2, 1, 3, 4), 2, 4, (nb, lb, gb * p))
    blockv = lambda v: block_diag(v.reshape(nb, gb, chunk, S5_GROUP, p), (0, 1, 4, 2, 3), 1, 4, (nb, gb * p, lb))
    sr, si = [pw_re[:, chunk]], [pw_im[:, chunk]]
    n_steps = max(1, int(math.log2(n_chunks))) if n_chunks > 1 else 1
    for _ in range(n_steps - 1):
        r0, i0 = sr[-1], si[-1]
        sr.append(r0 * r0 - i0 * i0)
        si.append(2.0 * r0 * i0)
    blockp = lambda x: jnp.stack(x, 1).reshape(nb, gb, len(x), p).transpose(0, 2, 1, 3).reshape(nb, len(x), gb * p)
    return (m, blockw(w_re), blockw(w_im), blockv(ca_re[:, 1:]), blockv(-ca_im[:, 1:]), blockp(sr), blockp(si))


def _s5_kernel(u_ref, m_ref, wre_ref, wim_ref, vre_ref, vim_ref, are_ref, aim_ref, h0re_ref, h0im_ref,
               y_ref, hre_ref, him_ref, *, chunk, n_chunks):
    if n_chunks == 1:
        xs = [u_ref[:, i, :] for i in range(chunk)]
    else:
        xs = [u_ref[0, pl.ds(i, n_chunks, stride=chunk), :] for i in range(chunk)]
    z_hi, z_lo = _split2(jnp.concatenate(xs, axis=1))
    y = _dot(z_hi, m_ref[0])
    s_re = _dot(z_hi, wre_ref[0]) + _dot(z_lo, wre_ref[0])
    s_im = _dot(z_hi, wim_ref[0]) + _dot(z_lo, wim_ref[0])
    pw_re, pw_im = are_ref[0], aim_ref[0]
    h0r, h0i = h0re_ref[:, 0, :], h0im_ref[:, 0, :]
    ar, ai = pw_re[0:1], pw_im[0:1]
    inj_re, inj_im = ar * h0r - ai * h0i, ar * h0i + ai * h0r
    if n_chunks == 1:
        s_re, s_im = s_re + inj_re, s_im + inj_im
        hs_re, hs_im = h0r, h0i
    else:
        row = lax.broadcasted_iota(jnp.int32, s_re.shape, 0)
        first = row == 0
        s_re = s_re + jnp.where(first, inj_re, 0.0)
        s_im = s_im + jnp.where(first, inj_im, 0.0)
        shift, j = 1, 0
        while shift < n_chunks:
            ar, ai = pw_re[j:j + 1], pw_im[j:j + 1]
            keep = row >= shift
            pr = jnp.where(keep, pltpu.roll(s_re, shift, 0), 0.0)
            pi = jnp.where(keep, pltpu.roll(s_im, shift, 0), 0.0)
            s_re, s_im = s_re + ar * pr - ai * pi, s_im + ar * pi + ai * pr
            shift, j = shift * 2, j + 1
        hs_re = jnp.where(first, h0r, pltpu.roll(s_re, 1, 0))
        hs_im = jnp.where(first, h0i, pltpu.roll(s_im, 1, 0))
    y = y + _dot(hs_re.astype(BF16), vre_ref[0]) + _dot(hs_im.astype(BF16), vim_ref[0])
    for i in range(chunk):
        yi = y[:, i * LANES:(i + 1) * LANES]
        if n_chunks == 1:
            y_ref[:, i, :] = yi
        else:
            y_ref[0, pl.ds(i, n_chunks, stride=chunk), :] = yi
    if n_chunks == 1:
        hre_ref[:, 0, :] = s_re
        him_ref[:, 0, :] = s_im
    else:
        hre_ref[0] = s_re[n_chunks - 1:n_chunks]
        him_ref[0] = s_im[n_chunks - 1:n_chunks]


def _s5(p3, ops, h0_re, h0_im, chunk):
    b, s, _ = p3.shape
    m, w_re, w_im, v_re, v_im, a_re, a_im = ops
    nblk, lb, pb = w_re.shape
    nc = s // chunk
    ns = a_re.shape[1]
    bb = b if nc == 1 else 1
    h0_re = h0_re.reshape(b, nblk, 1, pb)
    h0_im = h0_im.reshape(b, nblk, 1, pb)
    blk = lambda *shape: pl.BlockSpec((1,) + shape, lambda k, i: (k, 0, 0))
    st = pl.BlockSpec((bb, None, 1, pb), lambda k, i: (i, k, 0, 0))
    y, hre, him = pl.pallas_call(
        functools.partial(_s5_kernel, chunk=chunk, n_chunks=nc),
        grid=(nblk, b // bb),
        in_specs=[pl.BlockSpec((bb, s, LANES), lambda k, i: (i, 0, k)),
                  blk(lb, lb), blk(lb, pb), blk(lb, pb), blk(pb, lb), blk(pb, lb), blk(ns, pb), blk(ns, pb), st, st],
        out_specs=[pl.BlockSpec((bb, s, LANES), lambda k, i: (i, 0, k)), st, st],
        out_shape=[jax.ShapeDtypeStruct((b, s, nblk * LANES), F32), jax.ShapeDtypeStruct((b, nblk, 1, pb), F32),
                   jax.ShapeDtypeStruct((b, nblk, 1, pb), F32)],
        compiler_params=_cparams(("parallel", "arbitrary")),
        name="s5",
    )(p3, m, w_re, w_im, v_re, v_im, a_re, a_im, h0_re, h0_im)
    return y, hre.reshape(b, nblk, pb), him.reshape(b, nblk, pb)


def _gla_constants(tc):
    nl = int(math.log2(tc))
    t = np.arange(tc)
    blocks = []
    for j in range(nl):
        m = 1 << j
        base = t & ~(2 * m - 1)
        ref = base + m - 1
        col = np.arange(tc)[None, :]
        is_q = ((t >> j) & 1) == 1
        dq = (col > ref[:, None]) & (col <= t[:, None])
        dk = (col > t[:, None]) & (col <= ref[:, None])
        blocks.append(np.where(is_q[:, None], dq, dk))
    blocks.append(np.tril(np.ones((tc, tc), bool)))
    blocks.append(np.triu(np.ones((tc, tc), bool), 1))
    dall = np.concatenate(blocks, 0).astype(np.float32)
    x = t[:, None] ^ t[None, :]
    lvl = np.where(x > 0, np.floor(np.log2(np.maximum(x, 1))).astype(np.int32), -1)
    lvl = np.where(t[:, None] > t[None, :], lvl, np.where(t[:, None] == t[None, :], -1, -2)).astype(np.int32)
    return jnp.asarray(dall, BF16), jnp.asarray(lvl)


def _gla_kernel(q_ref, k_ref, v_ref, g_ref, r_ref, wgk_ref, bgk_ref, gn_ref, s0_ref, dall_ref, lvl_ref,
                o_ref, sfin_ref, st_ref, *, tc, valid):
    c = pl.program_id(1)
    nl = int(math.log2(tc))
    hdk = GLA_HEADS * GLA_DK

    @pl.when(c == 0)
    def _():
        st_ref[...] = s0_ref[0]

    q = q_ref[0] * (GLA_DK ** -0.5)
    k = k_ref[0]
    v = v_ref[0]
    x = _dot(r_ref[0].astype(BF16), wgk_ref[...]) + bgk_ref[...]
    gk = _log_sigmoid_pair(x)[0] / GLA_GATE_NORM
    if valid < tc:
        live = lax.broadcasted_iota(jnp.int32, gk.shape, 0) < valid
        gk = jnp.where(live, gk, 0.0)
        k = jnp.where(live, k, 0.0)
    g3 = jnp.concatenate(_split3(gk), axis=1)
    r3 = _dot(dall_ref[...], g3)
    ex = jnp.exp(r3[:, :hdk] + r3[:, hdk:2 * hdk] + r3[:, 2 * hdk:])
    e_cum = ex[nl * tc:(nl + 1) * tc]
    e_rem = ex[(nl + 1) * tc:]
    bl = _dot_tn(g3, jnp.ones((tc, GLA_DV), BF16))
    decay = jnp.exp(bl[:hdk] + bl[hdk:2 * hdk] + bl[2 * hdk:])

    row = lax.broadcasted_iota(jnp.int32, q.shape, 0)
    lane = lax.broadcasted_iota(jnp.int32, (tc, 2 * GLA_DK), 1)
    lvl = lvl_ref[...]
    q_in = (q * e_cum).astype(BF16)
    k_out = (k * e_rem).astype(BF16)
    xs = []
    for j in range(nl):
        is_q = ((row >> j) & 1) == 1
        xs.append((jnp.where(is_q, q, k) * ex[j * tc:(j + 1) * tc]).astype(BF16))
    qb, kb = q.astype(BF16), k.astype(BF16)
    gn = gn_ref[...]
    heads = range(GLA_HEADS)
    ps = [slice((h // 2) * 2 * GLA_DK, (h // 2 + 1) * 2 * GLA_DK) for h in heads]
    in_head = [(lane < GLA_DK) if h % 2 == 0 else (lane >= GLA_DK) for h in heads]
    g_diag = [_dot_nt(jnp.where(in_head[h], qb[:, ps[h]], 0), kb[:, ps[h]]) for h in heads]
    g_lvl = [[_dot_nt(jnp.where(in_head[h], xs[j][:, ps[h]], 0), xs[j][:, ps[h]]) for h in heads] for j in range(nl)]
    vh = [v[:, h * GLA_DV:(h + 1) * GLA_DV].astype(BF16) for h in heads]
    st = [st_ref[pair] for pair in range(GLA_HEADS // 2)]
    o_inter = [_dot(jnp.where(in_head[h], q_in[:, ps[h]], 0), st[h // 2].astype(BF16)) for h in heads]
    upd = [_dot_tn(k_out[:, ps[h]], vh[h]) for h in heads]
    att = []
    for h in heads:
        a = jnp.where(lvl == -1, g_diag[h], 0.0)
        for j in range(nl):
            a = jnp.where(lvl == j, g_lvl[j][h], a)
        att.append(a.astype(BF16))
    o_intra = [_dot(att[h], vh[h]) for h in heads]
    rsel = lax.broadcasted_iota(jnp.int32, (2 * GLA_DK, GLA_DV), 0)
    for pair in range(GLA_HEADS // 2):
        new = jnp.where(rsel < GLA_DK, upd[2 * pair], upd[2 * pair + 1])
        st_ref[pair] = decay[ps[2 * pair]] * st[pair] + new
    for h in heads:
        o = o_intra[h] + o_inter[h]
        o = o * lax.rsqrt(jnp.mean(o * o, axis=-1, keepdims=True) + EPS) * gn
        gh = g_ref[0, :, h * GLA_DV:(h + 1) * GLA_DV]
        o_ref[0, :, h * GLA_DV:(h + 1) * GLA_DV] = o * (gh * jax.nn.sigmoid(gh))

    sfin_ref[0] = st_ref[...]


def _gla(p3, w_gk, b_gk, gla_norm, s0, tc, valid):
    b, s, _ = p3.shape
    hdk, hdv = GLA_HEADS * GLA_DK, GLA_HEADS * GLA_DV
    u_w = 2 * hdk
    dall, lvl = _gla_constants(tc)
    const = lambda shape: pl.BlockSpec(shape, lambda i, j: tuple(0 for _ in shape))
    return pl.pallas_call(
        functools.partial(_gla_kernel, tc=tc, valid=valid),
        grid=(b, s // tc),
        in_specs=[pl.BlockSpec((1, tc, hdk), lambda i, j: (i, j, u_w // hdk)),
                  pl.BlockSpec((1, tc, hdk), lambda i, j: (i, j, u_w // hdk + 1)),
                  pl.BlockSpec((1, tc, hdv), lambda i, j: (i, j, (u_w + 2 * hdk) // hdv)),
                  pl.BlockSpec((1, tc, hdv), lambda i, j: (i, j, (u_w + 2 * hdk) // hdv + 1)),
                  pl.BlockSpec((1, tc, LANES), lambda i, j: (i, j, (u_w + 2 * hdk + 2 * hdv) // LANES)),
                  const((LANES, hdk)), const((1, hdk)), const((1, GLA_DV)),
                  pl.BlockSpec((1, GLA_HEADS // 2, 2 * GLA_DK, GLA_DV), lambda i, j: (i, 0, 0, 0)),
                  const(dall.shape), const(lvl.shape)],
        out_specs=[pl.BlockSpec((1, tc, hdv), lambda i, j: (i, j, 0)),
                   pl.BlockSpec((1, GLA_HEADS // 2, 2 * GLA_DK, GLA_DV), lambda i, j: (i, 0, 0, 0))],
        out_shape=[jax.ShapeDtypeStruct((b, s, hdv), F32),
                   jax.ShapeDtypeStruct((b, GLA_HEADS // 2, 2 * GLA_DK, GLA_DV), F32)],
        scratch_shapes=[pltpu.VMEM((GLA_HEADS // 2, 2 * GLA_DK, GLA_DV), F32)],
        compiler_params=_cparams(("parallel", "arbitrary")),
        name="gla",
    )(p3, p3, p3, p3, p3, w_gk, b_gk, gla_norm, s0, dall, lvl)


def _mix_out_kernel(x_ref, yc_ref, u_ref, og_ref, d_ref, wglu_ref, bglu_ref, wtop_ref, wbot_ref, o_ref):
    y = yc_ref[...] + d_ref[...] * u_ref[...]
    z = jax.nn.gelu(y)
    s5 = z * jax.nn.sigmoid(_dot(z.astype(BF16), wglu_ref[...]) + bglu_ref[...])
    o_ref[...] = (x_ref[...] + _dot(s5.astype(BF16), wtop_ref[...])
                  + _dot(og_ref[...].astype(BF16), wbot_ref[...]))


def _mix_out(x, y_conv, p, o_gla, d_skip, w_glu, b_glu, w_top, w_bot):
    t, d = x.shape
    w5 = y_conv.shape[1]
    tm = _tile(t, 512)
    row = lambda w: pl.BlockSpec((tm, w), lambda i: (i, 0))
    const = lambda a, b: pl.BlockSpec((a, b), lambda i: (0, 0))
    return pl.pallas_call(
        _mix_out_kernel,
        grid=(t // tm,),
        in_specs=[row(d), row(w5), row(w5), row(o_gla.shape[1]), const(1, w5), const(w5, w5), const(1, w5),
                  const(w5, d), const(o_gla.shape[1], d)],
        out_specs=row(d),
        out_shape=jax.ShapeDtypeStruct((t, d), F32),
        compiler_params=_cparams(("parallel",)),
        name="mix_out",
    )(x, y_conv, p, o_gla, d_skip, w_glu, b_glu, w_top, w_bot)


def _ffn_kernel(x_ref, gn_ref, hist_ref, wu_ref, wg_ref, wc_ref, bc_ref, wd_ref, gf_ref,
                o_ref, hist_out_ref, h_scr, acc_scr, pad_scr, carry_scr, *, rs, hist_rows, final_norm):
    t = pl.program_id(1)
    c = pl.program_id(2)
    tm = x_ref.shape[1]

    @pl.when(c == 0)
    def _():
        h_scr[...] = _rmsnorm(x_ref[0], gn_ref[...]).astype(BF16)
        acc_scr[...] = jnp.zeros_like(acc_scr)

    @pl.when(t == 0)
    def _():
        carry_scr[c] = hist_ref[0]

    h = h_scr[...]
    u = _dot(h, wu_ref[...])
    g = _dot(h, wg_ref[...])
    pad_scr[0:hist_rows] = carry_scr[c]
    pad_scr[hist_rows:] = u
    wc = wc_ref[...]
    uc = bc_ref[...]
    for j in range(CONV_W):
        off = hist_rows - (CONV_W - 1 - j) * rs
        uc = uc + wc[j:j + 1] * pad_scr[off:off + tm]
    new_hist = pad_scr[tm:tm + hist_rows]
    carry_scr[c] = new_hist
    hist_out_ref[0, 0] = new_hist
    hdn = jax.nn.gelu(uc) * g
    acc_scr[...] += _dot(hdn.astype(BF16), wd_ref[...])

    @pl.when(c == pl.num_programs(2) - 1)
    def _():
        y = x_ref[0] + acc_scr[...]
        if final_norm:
            y = _rmsnorm(y, gf_ref[...])
        o_ref[0] = y


def _ffn(x3, g_norm, hist, w_up, w_conv, b_conv, w_down, g_final, rs, final_norm):
    nb, s, d = x3.shape
    dff = w_down.shape[0]
    hist_rows = hist.shape[1]
    tm = _tile(s, 512)
    nt = s // tm
    ck = dff // 2 if (dff // 2) % LANES == 0 else dff
    nck = dff // ck
    y, hist_all = pl.pallas_call(
        functools.partial(_ffn_kernel, rs=rs, hist_rows=hist_rows, final_norm=final_norm),
        grid=(nb, s // tm, nck),
        in_specs=[pl.BlockSpec((1, tm, d), lambda b, t, c: (b, t, 0)),
                  pl.BlockSpec((1, d), lambda b, t, c: (0, 0)),
                  pl.BlockSpec((1, hist_rows, ck), lambda b, t, c: (b, 0, c)),
                  pl.BlockSpec((d, ck), lambda b, t, c: (0, c)),
                  pl.BlockSpec((d, ck), lambda b, t, c: (0, nck + c)),
                  pl.BlockSpec((CONV_W, ck), lambda b, t, c: (0, c)),
                  pl.BlockSpec((1, ck), lambda b, t, c: (0, c)),
                  pl.BlockSpec((ck, d), lambda b, t, c: (c, 0)),
                  pl.BlockSpec((1, d), lambda b, t, c: (0, 0))],
        out_specs=[pl.BlockSpec((1, tm, d), lambda b, t, c: (b, t, 0)),
                   pl.BlockSpec((1, 1, hist_rows, ck), lambda b, t, c: (b, t, 0, c))],
        out_shape=[jax.ShapeDtypeStruct((nb, s, d), F32), jax.ShapeDtypeStruct((nb, nt, hist_rows, dff), F32)],
        scratch_shapes=[pltpu.VMEM((tm, d), BF16), pltpu.VMEM((tm, d), F32),
                        pltpu.VMEM((hist_rows + tm, ck), F32), pltpu.VMEM((nck, hist_rows, ck), F32)],
        compiler_params=_cparams(("parallel", "arbitrary", "arbitrary")),
        name="ffn",
    )(x3, g_norm, hist, w_up, w_up, w_conv, b_conv, w_down, g_final)
    return y, hist_all[:, nt - 1]


def _sb_chains(zs, cbs, mo, mask):
    lsps, lons, firsts = [], [], []
    for chain in zs:
        for z in chain:
            lon = jnp.maximum(z, 0.0) + jnp.log2(1.0 + jnp.exp2(-jnp.abs(z)))
            lsps.append(z - lon)
            lon = lon if mask is None else jnp.where(mask, lon, 0.0)
            firsts.append(lon[:, :1])
            lons.append(lon.astype(BF16))
    rs = [_dot(lon, mo) for lon in lons]
    out, new_cbs, n = [], [], 0
    for chain, cb in zip(zs, cbs):
        weights = []
        for z in chain:
            tk = z.shape[1]
            a = jnp.exp2(lsps[n] + rs[n] + jnp.tile(cb, (1, tk // LANES)))
            weights.append((a if mask is None else jnp.where(mask, a, 0.0)).astype(BF16))
            cb = cb + jnp.broadcast_to(rs[n][:, :1] - firsts[n], cb.shape)
            n += 1
        out.append(weights)
        new_cbs.append(cb)
    return out, new_cbs


SB_TILES_PER_ITER = 4


def _sb_mo(tk):
    j = np.arange(tk)[:, None]
    s = np.arange(tk)[None, :]
    return jnp.asarray(-(j > s).astype(np.float32), BF16)


def _sb_prompt_kernel(bias_ref, q_ref, kt_ref, vt_ref, mo_ref, o_ref, acc_ref, cb_ref, *, tq):
    hp = pl.program_id(1)
    qi = pl.program_id(2)
    q = q_ref[0]
    lane = lax.broadcasted_iota(jnp.int32, q.shape, 1)
    tri = lax.broadcasted_iota(jnp.int32, (tq, tq), 1) < lax.broadcasted_iota(jnp.int32, (tq, tq), 0)
    mo = mo_ref[...]
    qh = [jnp.where(lane < SB_DIM, q, 0), jnp.where(lane >= SB_DIM, q, 0)]
    bias = [bias_ref[2 * hp], bias_ref[2 * hp + 1]]

    def tiles(kbs, mask, first):
        kts = [kt_ref[0, kb] for kb in kbs]
        zs = [[_dot(qh[sub], kt) + bias[sub] for kt in kts] for sub in range(2)]
        cbs = [jnp.zeros((tq, LANES), F32) if first else cb_ref[sub] for sub in range(2)]
        weights, cbs = _sb_chains(zs, cbs, mo, mask)
        vts = [vt_ref[0, kb] for kb in kbs]
        pvs = [[_dot_nt(a, vt) for a, vt in zip(weights[sub], vts)] for sub in range(2)]
        for sub in range(2):
            acc = sum(pvs[sub][1:], pvs[sub][0])
            acc_ref[sub] = acc if first else acc_ref[sub] + acc
            cb_ref[sub] = cbs[sub]

    tiles([qi], tri, True)

    def body(i, carry):
        kb = qi - 1 - SB_TILES_PER_ITER * i
        tiles([kb - n for n in range(SB_TILES_PER_ITER)], None, False)
        return carry

    lax.fori_loop(0, qi // SB_TILES_PER_ITER, body, 0)
    rem = qi % SB_TILES_PER_ITER
    n = SB_TILES_PER_ITER // 2
    while n:
        @pl.when((rem & n) != 0)
        def _(n=n):
            top = (rem & (2 * n - 1)) - 1
            tiles([top - m for m in range(n)], None, False)
        n //= 2

    o_ref[0] = jnp.where(lane < SB_DIM, acc_ref[0], acc_ref[1])


def _sb_prompt(q_b, kt_b, vt_b, bias2):
    b, s, hd = q_b.shape
    tq = kt_b.shape[3]
    nt = s // tq
    return pl.pallas_call(
        functools.partial(_sb_prompt_kernel, tq=tq),
        grid=(b, hd // LANES, nt),
        in_specs=[pl.BlockSpec(memory_space=pltpu.SMEM),
                  pl.BlockSpec((1, tq, LANES), lambda i, h, j: (i, j, h)),
                  pl.BlockSpec((1, nt, LANES, tq), lambda i, h, j: (i, 0, h, 0)),
                  pl.BlockSpec((1, nt, LANES, tq), lambda i, h, j: (i, 0, h, 0)),
                  pl.BlockSpec((tq, tq), lambda i, h, j: (0, 0))],
        out_specs=pl.BlockSpec((1, tq, LANES), lambda i, h, j: (i, j, h)),
        out_shape=jax.ShapeDtypeStruct((b, s, hd), F32),
        scratch_shapes=[pltpu.VMEM((2, tq, LANES), F32), pltpu.VMEM((2, tq, LANES), F32)],
        compiler_params=_cparams(("parallel", "parallel", "arbitrary")),
        name="sb_prompt",
    )(bias2, q_b, kt_b, vt_b, _sb_mo(tq))


SB_PAGES_PER_STEP = 8


def _sb_sample_kernel(pt_ref, qbd_ref, bias_ref, kn_ref, vn_ref, *rest, nq, npp):
    ck_refs, cv_refs = rest[:npp], rest[npp:2 * npp]
    mo_ref, o_ref, acc_ref, cb_ref = rest[2 * npp:]
    j = pl.program_id(1)
    qbd = qbd_ref[0]
    bias = bias_ref[...]
    mo = mo_ref[...]
    nh = SB_HEADS

    @pl.when(j == 0)
    def _():
        z = _dot_nt(qbd, kn_ref[0].astype(BF16)) + bias
        s_idx = lax.broadcasted_iota(jnp.int32, z.shape, 1)
        q_idx = lax.broadcasted_iota(jnp.int32, z.shape, 0) // nh
        w, cbs = _sb_chains([[z]], [jnp.zeros((z.shape[0], LANES), F32)], mo, s_idx < q_idx)
        acc_ref[...] = _dot(w[0][0], vn_ref[0].astype(BF16))
        cb_ref[...] = cbs[0]

    zs = [_dot(qbd, ck_refs[i][0, 0].astype(BF16)) + bias for i in range(npp)]
    w, cbs = _sb_chains([zs], [cb_ref[...]], mo, None)
    pvs = [_dot_nt(w[0][i], cv_refs[i][0, 0].astype(BF16)) for i in range(npp)]
    acc = acc_ref[...] + sum(pvs[1:], pvs[0])
    acc_ref[...] = acc
    cb_ref[...] = cbs[0]

    @pl.when(j == pl.num_programs(1) - 1)
    def _():
        r_h = lax.broadcasted_iota(jnp.int32, acc.shape, 0) % nh
        l_h = lax.broadcasted_iota(jnp.int32, acc.shape, 1) // SB_DIM
        sel = jnp.where(r_h == l_h, acc, 0.0)
        o_ref[0] = jnp.sum(sel.reshape(nq, nh, acc.shape[1]), axis=1)


def _sb_sample(q_b, k_new, v_new, cache_kt, cache_vt, layer, page_table, bias2):
    b, nq, hd = q_b.shape
    nh = SB_HEADS
    n_pages = page_table.shape[1]
    page = cache_kt.shape[3]
    npp = SB_PAGES_PER_STEP if n_pages % SB_PAGES_PER_STEP == 0 else 1
    qs = q_b.reshape(b, nq, nh, SB_DIM)
    eye = jnp.eye(nh, dtype=BF16)
    qbd = (qs[:, :, :, None, :] * eye[None, None, :, :, None]).reshape(b, nq * nh, hd)
    bias_b = jnp.broadcast_to(jnp.tile(bias2, nq)[:, None], (nq * nh, page))
    pad = lambda a: jnp.pad(a, ((0, 0), (0, page - nq), (0, 0)))
    page_spec = lambda i_page: pl.BlockSpec(
        (1, 1, hd, page), lambda i, j, pt: (layer, pt[i, n_pages - 1 - (j * npp + i_page)], 0, 0))
    return pl.pallas_call(
        functools.partial(_sb_sample_kernel, nq=nq, npp=npp),
        grid_spec=pltpu.PrefetchScalarGridSpec(
            num_scalar_prefetch=1,
            grid=(b, n_pages // npp),
            in_specs=([pl.BlockSpec((1, nq * nh, hd), lambda i, j, pt: (i, 0, 0)),
                       pl.BlockSpec((nq * nh, page), lambda i, j, pt: (0, 0)),
                       pl.BlockSpec((1, page, hd), lambda i, j, pt: (i, 0, 0)),
                       pl.BlockSpec((1, page, hd), lambda i, j, pt: (i, 0, 0))]
                      + [page_spec(i) for i in range(npp)] + [page_spec(i) for i in range(npp)]
                      + [pl.BlockSpec((page, page), lambda i, j, pt: (0, 0))]),
            out_specs=pl.BlockSpec((1, nq, hd), lambda i, j, pt: (i, 0, 0)),
            scratch_shapes=[pltpu.VMEM((nq * nh, hd), F32), pltpu.VMEM((nq * nh, LANES), F32)],
        ),
        out_shape=jax.ShapeDtypeStruct((b, nq, hd), F32),
        compiler_params=_cparams(("parallel", "arbitrary")),
        name="sb_sample",
    )(page_table, qbd, bias_b, pad(k_new), pad(v_new), *([cache_kt] * npp), *([cache_vt] * npp), _sb_mo(page))


def _even_layer(x3, g_mix, w_in_p, s5_ops, d_skip, w_glu, b_glu, w_gk_p, b_gk, gla_norm, w_top, w_bot,
                s5_h0, gla_s0, chunk):
    b, s, d = x3.shape
    s5_w = d_skip.shape[1]
    ng = s5_w // S5_GROUP
    x2 = x3.reshape(b * s, d)
    p = _inproj(x2, g_mix, w_in_p)
    p3 = p.reshape(b, s, p.shape[1])
    nblk, _, pb = s5_ops[1].shape
    if s5_h0 is None:
        h0r = h0i = jnp.zeros((b, nblk, pb), F32)
    else:
        h0r, h0i = s5_h0[0].reshape(b, nblk, pb), s5_h0[1].reshape(b, nblk, pb)
    y_conv, hre, him = _s5(p3, s5_ops, h0r, h0i, chunk)
    y_conv = y_conv.reshape(b * s, s5_w)
    fin = lambda a: a.reshape(b, ng, pb * nblk // ng)
    gla_tc = _tile(s, 128) if s >= 16 else 16
    gla_valid = min(s, gla_tc)
    sp = p3 if s >= gla_tc else jnp.pad(p3, ((0, 0), (0, gla_tc - s), (0, 0)))
    s0 = (jnp.zeros((b, GLA_HEADS // 2, 2 * GLA_DK, GLA_DV), F32) if gla_s0 is None
          else gla_s0.reshape(b, GLA_HEADS // 2, 2 * GLA_DK, GLA_DV))
    o_gla, s_fin = _gla(sp, w_gk_p, b_gk, gla_norm, s0, gla_tc, gla_valid)
    o_gla = o_gla[:, :s].reshape(b * s, o_gla.shape[2])
    x2 = _mix_out(x2, y_conv, p, o_gla, d_skip, w_glu, b_glu, w_top, w_bot)
    return x2.reshape(b, s, d), fin(hre), fin(him), s_fin.reshape(b, GLA_HEADS, GLA_DK, GLA_DV)


def kernel(x_prompt, x_sample, state_s5_re, state_s5_im, state_gla, cache_k, cache_v, state_ffn_conv, page_table, norm_mix, norm_ffn, norm_final, w_in0, s5_a_re, s5_a_im, s5_log_dt, s5_b_re, s5_b_im, s5_c_re, s5_c_im, s5_d, s5_w_glu, s5_b_glu, gla_w_gk, gla_b_gk, gla_norm, w_out0, w_qkv1, w_out1, sb_bias, ffn_w_up, ffn_w_conv, ffn_b_conv, ffn_w_down):
    bp, sp, d = x_prompt.shape
    bs, ss, _ = x_sample.shape
    depth = norm_mix.shape[0]
    dff = ffn_w_down.shape[1]
    s5_w = s5_d.shape[1]
    xp, xs = x_prompt, x_sample
    outs_p = {k: [] for k in ("s5r", "s5i", "gla", "k", "v", "conv")}
    outs_s = {k: [] for k in ("s5r", "s5i", "gla", "k", "v", "conv")}
    hist_rows_p = 8
    for l in range(depth):
        i = l // 2
        g_mix = norm_mix[l][None, :]
        if l % 2 == 0:
            in0 = w_in0.shape[2]
            in0p = -(-in0 // LANES) * LANES
            w_in_p = jnp.pad(w_in0[i], ((0, 0), (0, in0p - in0))).astype(BF16)
            w_gk_p = jnp.pad(gla_w_gk[i], ((0, LANES - GLA_RANK), (0, 0))).astype(BF16)
            common = (s5_d[i][None, :], s5_w_glu[i].astype(BF16), s5_b_glu[i][None, :], w_gk_p,
                      gla_b_gk[i][None, :], gla_norm[i][None, :], w_out0[i][:s5_w].astype(BF16),
                      w_out0[i][s5_w:].astype(BF16))
            s5_args = (s5_a_re[i], s5_a_im[i], s5_log_dt[i], s5_b_re[i], s5_b_im[i], s5_c_re[i], s5_c_im[i])
            chunk_p = _tile(sp, S5_CHUNK)
            ops_p = _s5_operators(*s5_args, chunk_p, sp // chunk_p)
            ops_s = _s5_operators(*s5_args, ss, 1)
            xp, hr, hi, sg = _even_layer(xp, g_mix, w_in_p, ops_p, *common, None, None, chunk_p)
            xs, hr2, hi2, sg2 = _even_layer(xs, g_mix, w_in_p, ops_s, *common,
                                            (state_s5_re[i], state_s5_im[i]), state_gla[i], ss)
            for o, vals in ((outs_p, (hr, hi, sg)), (outs_s, (hr2, hi2, sg2))):
                o["s5r"].append(vals[0])
                o["s5i"].append(vals[1])
                o["gla"].append(vals[2])
        else:
            w_qkv = w_qkv1[i].astype(BF16)
            w_out = w_out1[i].astype(BF16)
            bias2 = sb_bias[i] * LOG2E
            qb, kt, vt, ktb, vtb = _qkv_t(xp, g_mix, w_qkv[:, :d], w_qkv[:, d:].T, _tile(sp, 256))
            att = _sb_prompt(qb, ktb, vtb, bias2)
            xp = _proj_res(xp.reshape(bp * sp, d), att.reshape(bp * sp, d), w_out).reshape(bp, sp, d)
            qsb, ks, vs, _, _ = _qkv(xs.reshape(bs * ss, d), g_mix, w_qkv)
            as_t = lambda c: c.transpose(0, 1, 3, 4, 2).reshape(c.shape[0], c.shape[1], d, c.shape[2])
            att_s = _sb_sample(qsb.reshape(bs, ss, d), ks.reshape(bs, ss, d), vs.reshape(bs, ss, d),
                               as_t(cache_k), as_t(cache_v), i, page_table, bias2)
            xs = _proj_res(xs.reshape(bs * ss, d), att_s.reshape(bs * ss, d), w_out).reshape(bs, ss, d)
            outs_p["k"].append(kt.reshape(bp, SB_HEADS, SB_DIM, sp).transpose(0, 3, 1, 2))
            outs_p["v"].append(vt.reshape(bp, SB_HEADS, SB_DIM, sp).transpose(0, 3, 1, 2))
            outs_s["k"].append(ks.reshape(bs, ss, SB_HEADS, SB_DIM))
            outs_s["v"].append(vs.reshape(bs, ss, SB_HEADS, SB_DIM))
        final = l == depth - 1
        w_up = ffn_w_up[l].astype(BF16)
        w_down = ffn_w_down[l].astype(BF16)
        ffn_args = (w_up, ffn_w_conv[l], ffn_b_conv[l][None, :], w_down, norm_final[None, :])
        xp, hist_p = _ffn(xp, norm_ffn[l][None, :], jnp.zeros((bp, hist_rows_p, dff), F32), *ffn_args, 1, final)
        outs_p["conv"].append(hist_p[:, hist_rows_p - (CONV_W - 1):])
        hist_s = state_ffn_conv[l].transpose(1, 0, 2).reshape(1, (CONV_W - 1) * bs, dff)
        xs_t, hist_s = _ffn(xs.transpose(1, 0, 2).reshape(1, ss * bs, d), norm_ffn[l][None, :], hist_s,
                            *ffn_args, bs, final)
        xs = xs_t.reshape(ss, bs, d).transpose(1, 0, 2)
        outs_s["conv"].append(hist_s.reshape(CONV_W - 1, bs, dff).transpose(1, 0, 2))
    st = jnp.stack
    return (xp, xs,
            st(outs_p["s5r"]), st(outs_p["s5i"]), st(outs_p["gla"]), st(outs_p["k"]), st(outs_p["v"]), st(outs_p["conv"]),
            st(outs_s["s5r"]), st(outs_s["s5i"]), st(outs_s["gla"]), st(outs_s["k"]), st(outs_s["v"]), st(outs_s["conv"]))
```

```python
import functools
import math

import jax
import jax.numpy as jnp
import numpy as np
from jax import lax
from jax.experimental import pallas as pl
from jax.experimental.pallas import tpu as pltpu

F32 = jnp.float32
BF16 = jnp.bfloat16
EPS = 1e-6

S5_GROUP = 16
S5_CHUNK = 16
GLA_HEADS = 4
GLA_DK = 64
GLA_DV = 128
GLA_RANK = 16
GLA_GATE_NORM = 16.0
SB_HEADS = 16
SB_DIM = 64
CONV_W = 3
LOG2E = math.log2(math.e)
SB_QSCALE = SB_DIM ** -0.5 * LOG2E
LANES = 128
VMEM_LIMIT = 56 * 1024 * 1024


def _cparams(sem):
    return pltpu.CompilerParams(dimension_semantics=sem, vmem_limit_bytes=VMEM_LIMIT)


def _tile(n, pref):
    if n <= pref:
        return n
    t = pref
    while n % t:
        t //= 2
    return t


def _rmsnorm(x, g):
    y = x * lax.rsqrt(jnp.mean(x * x, axis=-1, keepdims=True) + EPS)
    return y * g


def _split2(x):
    hi = x.astype(BF16)
    lo = (x - hi.astype(F32)).astype(BF16)
    return hi, lo


def _split3(x):
    hi = x.astype(BF16)
    r = x - hi.astype(F32)
    mid = r.astype(BF16)
    lo = (r - mid.astype(F32)).astype(BF16)
    return hi, mid, lo


def _dot(a, b):
    return jnp.dot(a, b, preferred_element_type=F32)


def _dot_nt(a, b):
    return lax.dot_general(a, b, (((1,), (1,)), ((), ())), preferred_element_type=F32)


def _dot_tn(a, b):
    return lax.dot_general(a, b, (((0,), (0,)), ((), ())), preferred_element_type=F32)


def _log_sigmoid_pair(z):
    lsp = jnp.minimum(z, 0.0) - jnp.log1p(jnp.exp(-jnp.abs(z)))
    return lsp, lsp - z


def _inproj_kernel(x_ref, g_ref, w_ref, o_ref):
    h = _rmsnorm(x_ref[...], g_ref[...]).astype(BF16)
    o_ref[...] = _dot(h, w_ref[...])


def _inproj(x, g, w):
    t, d = x.shape
    n = w.shape[1]
    tm = _tile(t, 512)
    return pl.pallas_call(
        _inproj_kernel,
        grid=(t // tm,),
        in_specs=[pl.BlockSpec((tm, d), lambda i: (i, 0)),
                  pl.BlockSpec((1, d), lambda i: (0, 0)),
                  pl.BlockSpec((d, n), lambda i: (0, 0))],
        out_specs=pl.BlockSpec((tm, n), lambda i: (i, 0)),
        out_shape=jax.ShapeDtypeStruct((t, n), F32),
        compiler_params=_cparams(("parallel",)),
        name="inproj",
    )(x, g, w)


def _qkv_kernel(x_ref, g_ref, w_ref, qb_ref, k_ref, v_ref, kb_ref, vb_ref):
    d = x_ref.shape[1]
    h = _rmsnorm(x_ref[...], g_ref[...]).astype(BF16)
    p = _dot(h, w_ref[...])
    q, k, v = p[:, :d], p[:, d:2 * d], p[:, 2 * d:]
    qb_ref[...] = (q * SB_QSCALE).astype(BF16)
    k_ref[...] = k
    v_ref[...] = v
    kb_ref[...] = k.astype(BF16)
    vb_ref[...] = v.astype(BF16)


def _qkv(x, g, w):
    t, d = x.shape
    tm = _tile(t, 256)
    row = pl.BlockSpec((tm, d), lambda i: (i, 0))
    return pl.pallas_call(
        _qkv_kernel,
        grid=(t // tm,),
        in_specs=[row, pl.BlockSpec((1, d), lambda i: (0, 0)), pl.BlockSpec((d, 3 * d), lambda i: (0, 0))],
        out_specs=[row, row, row, row, row],
        out_shape=[jax.ShapeDtypeStruct((t, d), BF16), jax.ShapeDtypeStruct((t, d), F32),
                   jax.ShapeDtypeStruct((t, d), F32), jax.ShapeDtypeStruct((t, d), BF16),
                   jax.ShapeDtypeStruct((t, d), BF16)],
        compiler_params=_cparams(("parallel",)),
        name="qkv",
    )(x, g, w)


def _qkv_t_kernel(x_ref, g_ref, wq_ref, wkvt_ref, qb_ref, kt_ref, vt_ref, ktb_ref, vtb_ref):
    d = x_ref.shape[2]
    h = _rmsnorm(x_ref[0], g_ref[...]).astype(BF16)
    qb_ref[0] = (_dot(h, wq_ref[...]) * SB_QSCALE).astype(BF16)
    kvt = _dot_nt(wkvt_ref[...], h)
    kt_ref[0] = kvt[:d]
    vt_ref[0] = kvt[d:]
    ktb_ref[0, 0] = kvt[:d].astype(BF16)
    vtb_ref[0, 0] = kvt[d:].astype(BF16)


def _qkv_t(x3, g, wq, wkvt, tm):
    b, s, d = x3.shape
    nt = s // tm
    return pl.pallas_call(
        _qkv_t_kernel,
        grid=(b, nt),
        in_specs=[pl.BlockSpec((1, tm, d), lambda i, t: (i, t, 0)),
                  pl.BlockSpec((1, d), lambda i, t: (0, 0)),
                  pl.BlockSpec((d, d), lambda i, t: (0, 0)),
                  pl.BlockSpec((2 * d, d), lambda i, t: (0, 0))],
        out_specs=[pl.BlockSpec((1, tm, d), lambda i, t: (i, t, 0)),
                   pl.BlockSpec((1, d, tm), lambda i, t: (i, 0, t)),
                   pl.BlockSpec((1, d, tm), lambda i, t: (i, 0, t)),
                   pl.BlockSpec((1, 1, d, tm), lambda i, t: (i, t, 0, 0)),
                   pl.BlockSpec((1, 1, d, tm), lambda i, t: (i, t, 0, 0))],
        out_shape=[jax.ShapeDtypeStruct((b, s, d), BF16), jax.ShapeDtypeStruct((b, d, s), F32),
                   jax.ShapeDtypeStruct((b, d, s), F32), jax.ShapeDtypeStruct((b, nt, d, tm), BF16),
                   jax.ShapeDtypeStruct((b, nt, d, tm), BF16)],
        compiler_params=_cparams(("parallel", "parallel")),
        name="qkv_t",
    )(x3, g, wq, wkvt)


def _proj_res_kernel(x_ref, a_ref, w_ref, o_ref):
    o_ref[...] = x_ref[...] + _dot(a_ref[...].astype(BF16), w_ref[...])


def _proj_res(x, a, w):
    t, d = x.shape
    k = a.shape[1]
    tm = _tile(t, 512)
    return pl.pallas_call(
        _proj_res_kernel,
        grid=(t // tm,),
        in_specs=[pl.BlockSpec((tm, d), lambda i: (i, 0)),
                  pl.BlockSpec((tm, k), lambda i: (i, 0)),
                  pl.BlockSpec((k, d), lambda i: (0, 0))],
        out_specs=pl.BlockSpec((tm, d), lambda i: (i, 0)),
        out_shape=jax.ShapeDtypeStruct((t, d), F32),
        compiler_params=_cparams(("parallel",)),
        name="proj_res",
    )(x, a, w)


S5_BLOCK_GROUPS = LANES // S5_GROUP


def _s5_operators(a_re, a_im, log_dt, b_re, b_im, c_re, c_im, chunk, n_chunks):
    hp = lax.Precision.HIGHEST
    g, p = a_re.shape
    dt = jnp.exp(log_dt)[:, None]
    mag = jnp.exp(a_re * dt)
    ab_re, ab_im = mag * jnp.cos(a_im * dt), mag * jnp.sin(a_im * dt)
    den = a_re * a_re + a_im * a_im
    nr, ni = ab_re - 1.0, ab_im
    coef_re = (nr * a_re + ni * a_im) / den
    coef_im = (ni * a_re - nr * a_im) / den
    bb_re = coef_re[..., None] * b_re - coef_im[..., None] * b_im
    bb_im = coef_re[..., None] * b_im + coef_im[..., None] * b_re
    pr, pi = [jnp.ones_like(ab_re)], [jnp.zeros_like(ab_re)]
    for _ in range(chunk):
        r0, i0 = pr[-1], pi[-1]
        pr.append(r0 * ab_re - i0 * ab_im)
        pi.append(r0 * ab_im + i0 * ab_re)
    pw_re, pw_im = jnp.stack(pr, 1), jnp.stack(pi, 1)
    ca_re = c_re[:, None] * pw_re[:, :, None, :] - c_im[:, None] * pw_im[:, :, None, :]
    ca_im = c_re[:, None] * pw_im[:, :, None, :] + c_im[:, None] * pw_re[:, :, None, :]
    taps = (jnp.einsum('gtcp,gpd->gtcd', ca_re[:, :chunk], bb_re, precision=hp)
            - jnp.einsum('gtcp,gpd->gtcd', ca_im[:, :chunk], bb_im, precision=hp))
    gb = S5_BLOCK_GROUPS
    nb = g // gb
    lb = chunk * LANES

    def group_diag(x):
        n, _, r, w = x.shape
        tiled = jnp.dot(x.reshape(n * gb * r, w), np.tile(np.eye(w, dtype=np.float32), (1, gb)), precision=hp)
        same = (np.arange(gb * r)[:, None] // r) == (np.arange(gb * w)[None, :] // w)
        return jnp.where(same[None], tiled.reshape(n, gb * r, gb * w), 0.0).astype(BF16)

    taps_rev = taps[:, np.arange(chunk - 1, -1, -1)].reshape(nb, gb, chunk, S5_GROUP, S5_GROUP)
    t_b = group_diag(taps_rev.transpose(0, 2, 1, 4, 3).reshape(nb * chunk, gb, S5_GROUP, S5_GROUP)).reshape(nb, lb, LANES)
    t_a = jnp.concatenate([t_b[:, LANES:], jnp.zeros((nb, LANES, LANES), BF16)], axis=1)
    m = jnp.concatenate([t_a, t_b], axis=2)
    rev_re, rev_im = jnp.stack(pr[chunk - 1::-1], 1), jnp.stack(pi[chunk - 1::-1], 1)
    bt_re, bt_im = bb_re.transpose(0, 2, 1)[:, None], bb_im.transpose(0, 2, 1)[:, None]
    w_re = rev_re[:, :, None, :] * bt_re - rev_im[:, :, None, :] * bt_im
    w_im = rev_re[:, :, None, :] * bt_im + rev_im[:, :, None, :] * bt_re
    blockw = lambda w: group_diag(w.reshape(nb, gb, chunk, S5_GROUP, p).transpose(0, 2, 1, 3, 4)
                                  .reshape(nb * chunk, gb, S5_GROUP, p)).reshape(nb, lb, gb * p)
    blockv = lambda v: (group_diag(v.reshape(nb, gb, chunk, S5_GROUP, p).transpose(0, 2, 1, 4, 3)
                                   .reshape(nb * chunk, gb, p, S5_GROUP))
                        .reshape(nb, chunk, gb * p, LANES).transpose(0, 2, 1, 3).reshape(nb, gb * p, lb))
    sr, si = [pw_re[:, chunk]], [pw_im[:, chunk]]
    n_steps = max(1, int(math.log2(n_chunks))) if n_chunks > 1 else 1
    for _ in range(n_steps - 1):
        r0, i0 = sr[-1], si[-1]
        sr.append(r0 * r0 - i0 * i0)
        si.append(2.0 * r0 * i0)
    blockp = lambda x: jnp.stack(x, 1).reshape(nb, gb, len(x), p).transpose(0, 2, 1, 3).reshape(nb, len(x), gb * p)
    return (m, blockw(w_re), blockw(w_im), blockv(ca_re[:, 1:]), blockv(-ca_im[:, 1:]), blockp(sr), blockp(si))


def _s5_kernel(u_ref, m_ref, wre_ref, wim_ref, vre_ref, vim_ref, are_ref, aim_ref, h0re_ref, h0im_ref,
               y_ref, hre_ref, him_ref, *, chunk, n_chunks):
    if n_chunks == 1:
        xs = [u_ref[:, i, :] for i in range(chunk)]
    else:
        xs = [u_ref[0, pl.ds(i, n_chunks, stride=chunk), :] for i in range(chunk)]
    z_hi, z_lo = _split2(jnp.concatenate(xs, axis=1))
    lb = chunk * LANES
    y_pairs = [_dot(z_hi[:, :2 * LANES * (k + 1)], m_ref[0, lb - 2 * LANES * (k + 1):lb, :]) for k in range(chunk // 2)]
    s_re = _dot(z_hi, wre_ref[0]) + _dot(z_lo, wre_ref[0])
    s_im = _dot(z_hi, wim_ref[0]) + _dot(z_lo, wim_ref[0])
    pw_re, pw_im = are_ref[0], aim_ref[0]
    h0r, h0i = h0re_ref[:, 0, :], h0im_ref[:, 0, :]
    ar, ai = pw_re[0:1], pw_im[0:1]
    inj_re, inj_im = ar * h0r - ai * h0i, ar * h0i + ai * h0r
    if n_chunks == 1:
        s_re, s_im = s_re + inj_re, s_im + inj_im
        hs_re, hs_im = h0r, h0i
    else:
        row = lax.broadcasted_iota(jnp.int32, s_re.shape, 0)
        first = row == 0
        s_re = s_re + jnp.where(first, inj_re, 0.0)
        s_im = s_im + jnp.where(first, inj_im, 0.0)
        shift, j = 1, 0
        while shift < n_chunks:
            ar, ai = pw_re[j:j + 1], pw_im[j:j + 1]
            keep = row >= shift
            pr = jnp.where(keep, pltpu.roll(s_re, shift, 0), 0.0)
            pi = jnp.where(keep, pltpu.roll(s_im, shift, 0), 0.0)
            s_re, s_im = s_re + ar * pr - ai * pi, s_im + ar * pi + ai * pr
            shift, j = shift * 2, j + 1
        hs_re = jnp.where(first, h0r, pltpu.roll(s_re, 1, 0))
        hs_im = jnp.where(first, h0i, pltpu.roll(s_im, 1, 0))
    y = _dot(hs_re.astype(BF16), vre_ref[0]) + _dot(hs_im.astype(BF16), vim_ref[0])
    for i in range(chunk):
        yi = y[:, i * LANES:(i + 1) * LANES] + y_pairs[i // 2][:, (i % 2) * LANES:(i % 2 + 1) * LANES]
        if n_chunks == 1:
            y_ref[:, i, :] = yi
        else:
            y_ref[0, pl.ds(i, n_chunks, stride=chunk), :] = yi
    if n_chunks == 1:
        hre_ref[:, 0, :] = s_re
        him_ref[:, 0, :] = s_im
    else:
        hre_ref[0] = s_re[n_chunks - 1:n_chunks]
        him_ref[0] = s_im[n_chunks - 1:n_chunks]


def _s5(p3, ops, h0_re, h0_im, chunk):
    b, s, _ = p3.shape
    m, w_re, w_im, v_re, v_im, a_re, a_im = ops
    nblk, lb, pb = w_re.shape
    nc = s // chunk
    ns = a_re.shape[1]
    bb = b if nc == 1 else 1
    h0_re = h0_re.reshape(b, nblk, 1, pb)
    h0_im = h0_im.reshape(b, nblk, 1, pb)
    blk = lambda *shape: pl.BlockSpec((1,) + shape, lambda k, i: (k, 0, 0))
    st = pl.BlockSpec((bb, None, 1, pb), lambda k, i: (i, k, 0, 0))
    y, hre, him = pl.pallas_call(
        functools.partial(_s5_kernel, chunk=chunk, n_chunks=nc),
        grid=(nblk, b // bb),
        in_specs=[pl.BlockSpec((bb, s, LANES), lambda k, i: (i, 0, k)),
                  blk(lb, 2 * LANES), blk(lb, pb), blk(lb, pb), blk(pb, lb), blk(pb, lb), blk(ns, pb), blk(ns, pb),
                  st, st],
        out_specs=[pl.BlockSpec((bb, s, LANES), lambda k, i: (i, 0, k)), st, st],
        out_shape=[jax.ShapeDtypeStruct((b, s, nblk * LANES), F32), jax.ShapeDtypeStruct((b, nblk, 1, pb), F32),
                   jax.ShapeDtypeStruct((b, nblk, 1, pb), F32)],
        compiler_params=_cparams(("parallel", "arbitrary")),
        name="s5",
    )(p3, m, w_re, w_im, v_re, v_im, a_re, a_im, h0_re, h0_im)
    return y, hre.reshape(b, nblk, pb), him.reshape(b, nblk, pb)


def _gla_constants(tc):
    nl = int(math.log2(tc))
    t = np.arange(tc)
    blocks = []
    for j in range(nl):
        m = 1 << j
        base = t & ~(2 * m - 1)
        ref = base + m - 1
        col = np.arange(tc)[None, :]
        is_q = ((t >> j) & 1) == 1
        dq = (col > ref[:, None]) & (col <= t[:, None])
        dk = (col > t[:, None]) & (col <= ref[:, None])
        blocks.append(np.where(is_q[:, None], dq, dk))
    blocks.append(np.tril(np.ones((tc, tc), bool)))
    blocks.append(np.triu(np.ones((tc, tc), bool), 1))
    dall = np.concatenate(blocks, 0).astype(np.float32)
    x = t[:, None] ^ t[None, :]
    lvl = np.where(x > 0, np.floor(np.log2(np.maximum(x, 1))).astype(np.int32), -1)
    lvl = np.where(t[:, None] > t[None, :], lvl, np.where(t[:, None] == t[None, :], -1, -2)).astype(np.int32)
    return jnp.asarray(dall, BF16), jnp.asarray(lvl)


def _gla_kernel(q_ref, k_ref, v_ref, g_ref, r_ref, wgk_ref, bgk_ref, gn_ref, s0_ref, dall_ref, lvl_ref,
                o_ref, sfin_ref, st_ref, *, tc, valid):
    c = pl.program_id(1)
    nl = int(math.log2(tc))
    hdk = GLA_HEADS * GLA_DK

    @pl.when(c == 0)
    def _():
        st_ref[...] = s0_ref[0]

    q = q_ref[0] * (GLA_DK ** -0.5)
    k = k_ref[0]
    v = v_ref[0]
    x = _dot(r_ref[0].astype(BF16), wgk_ref[...]) + bgk_ref[...]
    gk = _log_sigmoid_pair(x)[0] / GLA_GATE_NORM
    if valid < tc:
        live = lax.broadcasted_iota(jnp.int32, gk.shape, 0) < valid
        gk = jnp.where(live, gk, 0.0)
        k = jnp.where(live, k, 0.0)
    g3 = jnp.concatenate(_split3(gk), axis=1)
    r3 = _dot(dall_ref[...], g3)
    ex = jnp.exp(r3[:, :hdk] + r3[:, hdk:2 * hdk] + r3[:, 2 * hdk:])
    e_cum = ex[nl * tc:(nl + 1) * tc]
    e_rem = ex[(nl + 1) * tc:]
    bl = _dot_tn(g3, jnp.ones((tc, GLA_DV), BF16))
    decay = jnp.exp(bl[:hdk] + bl[hdk:2 * hdk] + bl[2 * hdk:])

    row = lax.broadcasted_iota(jnp.int32, q.shape, 0)
    lane = lax.broadcasted_iota(jnp.int32, (tc, 2 * GLA_DK), 1)
    lvl = lvl_ref[...]
    q_in = (q * e_cum).astype(BF16)
    k_out = (k * e_rem).astype(BF16)
    xs = []
    for j in range(nl):
        is_q = ((row >> j) & 1) == 1
        xs.append((jnp.where(is_q, q, k) * ex[j * tc:(j + 1) * tc]).astype(BF16))
    qb, kb = q.astype(BF16), k.astype(BF16)
    gn = gn_ref[...]
    heads = range(GLA_HEADS)
    ps = [slice((h // 2) * 2 * GLA_DK, (h // 2 + 1) * 2 * GLA_DK) for h in heads]
    in_head = [(lane < GLA_DK) if h % 2 == 0 else (lane >= GLA_DK) for h in heads]
    g_diag = [_dot_nt(jnp.where(in_head[h], qb[:, ps[h]], 0), kb[:, ps[h]]) for h in heads]
    g_lvl = [[_dot_nt(jnp.where(in_head[h], xs[j][:, ps[h]], 0), xs[j][:, ps[h]]) for h in heads] for j in range(nl)]
    vh = [v[:, h * GLA_DV:(h + 1) * GLA_DV].astype(BF16) for h in heads]
    st = [st_ref[pair] for pair in range(GLA_HEADS // 2)]
    o_inter = [_dot(jnp.where(in_head[h], q_in[:, ps[h]], 0), st[h // 2].astype(BF16)) for h in heads]
    upd = [_dot_tn(k_out[:, ps[h]], vh[h]) for h in heads]
    att = []
    for h in heads:
        a = jnp.where(lvl == -1, g_diag[h], 0.0)
        for j in range(nl):
            a = jnp.where(lvl == j, g_lvl[j][h], a)
        att.append(a.astype(BF16))
    o_intra = [_dot(att[h], vh[h]) for h in heads]
    rsel = lax.broadcasted_iota(jnp.int32, (2 * GLA_DK, GLA_DV), 0)
    for pair in range(GLA_HEADS // 2):
        new = jnp.where(rsel < GLA_DK, upd[2 * pair], upd[2 * pair + 1])
        st_ref[pair] = decay[ps[2 * pair]] * st[pair] + new
    for h in heads:
        o = o_intra[h] + o_inter[h]
        o = o * lax.rsqrt(jnp.mean(o * o, axis=-1, keepdims=True) + EPS) * gn
        gh = g_ref[0, :, h * GLA_DV:(h + 1) * GLA_DV]
        o_ref[0, :, h * GLA_DV:(h + 1) * GLA_DV] = o * (gh * jax.nn.sigmoid(gh))

    sfin_ref[0] = st_ref[...]


def _gla(p3, w_gk, b_gk, gla_norm, s0, tc, valid):
    b, s, _ = p3.shape
    hdk, hdv = GLA_HEADS * GLA_DK, GLA_HEADS * GLA_DV
    u_w = 2 * hdk
    dall, lvl = _gla_constants(tc)
    const = lambda shape: pl.BlockSpec(shape, lambda i, j: tuple(0 for _ in shape))
    return pl.pallas_call(
        functools.partial(_gla_kernel, tc=tc, valid=valid),
        grid=(b, s // tc),
        in_specs=[pl.BlockSpec((1, tc, hdk), lambda i, j: (i, j, u_w // hdk)),
                  pl.BlockSpec((1, tc, hdk), lambda i, j: (i, j, u_w // hdk + 1)),
                  pl.BlockSpec((1, tc, hdv), lambda i, j: (i, j, (u_w + 2 * hdk) // hdv)),
                  pl.BlockSpec((1, tc, hdv), lambda i, j: (i, j, (u_w + 2 * hdk) // hdv + 1)),
                  pl.BlockSpec((1, tc, LANES), lambda i, j: (i, j, (u_w + 2 * hdk + 2 * hdv) // LANES)),
                  const((LANES, hdk)), const((1, hdk)), const((1, GLA_DV)),
                  pl.BlockSpec((1, GLA_HEADS // 2, 2 * GLA_DK, GLA_DV), lambda i, j: (i, 0, 0, 0)),
                  const(dall.shape), const(lvl.shape)],
        out_specs=[pl.BlockSpec((1, tc, hdv), lambda i, j: (i, j, 0)),
                   pl.BlockSpec((1, GLA_HEADS // 2, 2 * GLA_DK, GLA_DV), lambda i, j: (i, 0, 0, 0))],
        out_shape=[jax.ShapeDtypeStruct((b, s, hdv), F32),
                   jax.ShapeDtypeStruct((b, GLA_HEADS // 2, 2 * GLA_DK, GLA_DV), F32)],
        scratch_shapes=[pltpu.VMEM((GLA_HEADS // 2, 2 * GLA_DK, GLA_DV), F32)],
        compiler_params=_cparams(("parallel", "arbitrary")),
        name="gla",
    )(p3, p3, p3, p3, p3, w_gk, b_gk, gla_norm, s0, dall, lvl)


def _mix_out_kernel(x_ref, yc_ref, u_ref, og_ref, d_ref, wglu_ref, bglu_ref, wtop_ref, wbot_ref, o_ref):
    y = yc_ref[...] + d_ref[...] * u_ref[...]
    z = jax.nn.gelu(y)
    s5 = z * jax.nn.sigmoid(_dot(z.astype(BF16), wglu_ref[...]) + bglu_ref[...])
    o_ref[...] = (x_ref[...] + _dot(s5.astype(BF16), wtop_ref[...])
                  + _dot(og_ref[...].astype(BF16), wbot_ref[...]))


def _mix_out(x, y_conv, p, o_gla, d_skip, w_glu, b_glu, w_top, w_bot):
    t, d = x.shape
    w5 = y_conv.shape[1]
    tm = _tile(t, 512)
    row = lambda w: pl.BlockSpec((tm, w), lambda i: (i, 0))
    const = lambda a, b: pl.BlockSpec((a, b), lambda i: (0, 0))
    return pl.pallas_call(
        _mix_out_kernel,
        grid=(t // tm,),
        in_specs=[row(d), row(w5), row(w5), row(o_gla.shape[1]), const(1, w5), const(w5, w5), const(1, w5),
                  const(w5, d), const(o_gla.shape[1], d)],
        out_specs=row(d),
        out_shape=jax.ShapeDtypeStruct((t, d), F32),
        compiler_params=_cparams(("parallel",)),
        name="mix_out",
    )(x, y_conv, p, o_gla, d_skip, w_glu, b_glu, w_top, w_bot)


def _ffn_kernel(x_ref, gn_ref, hist_ref, wu_ref, wg_ref, wc_ref, bc_ref, wd_ref, gf_ref,
                o_ref, hist_out_ref, h_scr, acc_scr, pad_scr, carry_scr, *, rs, hist_rows, final_norm):
    t = pl.program_id(1)
    c = pl.program_id(2)
    tm = x_ref.shape[1]

    @pl.when(c == 0)
    def _():
        h_scr[...] = _rmsnorm(x_ref[0], gn_ref[...]).astype(BF16)
        acc_scr[...] = jnp.zeros_like(acc_scr)

    @pl.when(t == 0)
    def _():
        carry_scr[c] = hist_ref[0]

    h = h_scr[...]
    u = _dot(h, wu_ref[...])
    g = _dot(h, wg_ref[...])
    pad_scr[0:hist_rows] = carry_scr[c]
    pad_scr[hist_rows:] = u
    wc = wc_ref[...]
    uc = bc_ref[...]
    for j in range(CONV_W):
        off = hist_rows - (CONV_W - 1 - j) * rs
        uc = uc + wc[j:j + 1] * pad_scr[off:off + tm]
    new_hist = pad_scr[tm:tm + hist_rows]
    carry_scr[c] = new_hist
    hist_out_ref[0, 0] = new_hist
    hdn = jax.nn.gelu(uc) * g
    acc_scr[...] += _dot(hdn.astype(BF16), wd_ref[...])

    @pl.when(c == pl.num_programs(2) - 1)
    def _():
        y = x_ref[0] + acc_scr[...]
        if final_norm:
            y = _rmsnorm(y, gf_ref[...])
        o_ref[0] = y


def _ffn(x3, g_norm, hist, w_up, w_conv, b_conv, w_down, g_final, rs, final_norm):
    nb, s, d = x3.shape
    dff = w_down.shape[0]
    hist_rows = hist.shape[1]
    tm = _tile(s, 512)
    nt = s // tm
    ck = dff // 2 if (dff // 2) % LANES == 0 else dff
    nck = dff // ck
    y, hist_all = pl.pallas_call(
        functools.partial(_ffn_kernel, rs=rs, hist_rows=hist_rows, final_norm=final_norm),
        grid=(nb, s // tm, nck),
        in_specs=[pl.BlockSpec((1, tm, d), lambda b, t, c: (b, t, 0)),
                  pl.BlockSpec((1, d), lambda b, t, c: (0, 0)),
                  pl.BlockSpec((1, hist_rows, ck), lambda b, t, c: (b, 0, c)),
                  pl.BlockSpec((d, ck), lambda b, t, c: (0, c)),
                  pl.BlockSpec((d, ck), lambda b, t, c: (0, nck + c)),
                  pl.BlockSpec((CONV_W, ck), lambda b, t, c: (0, c)),
                  pl.BlockSpec((1, ck), lambda b, t, c: (0, c)),
                  pl.BlockSpec((ck, d), lambda b, t, c: (c, 0)),
                  pl.BlockSpec((1, d), lambda b, t, c: (0, 0))],
        out_specs=[pl.BlockSpec((1, tm, d), lambda b, t, c: (b, t, 0)),
                   pl.BlockSpec((1, 1, hist_rows, ck), lambda b, t, c: (b, t, 0, c))],
        out_shape=[jax.ShapeDtypeStruct((nb, s, d), F32), jax.ShapeDtypeStruct((nb, nt, hist_rows, dff), F32)],
        scratch_shapes=[pltpu.VMEM((tm, d), BF16), pltpu.VMEM((tm, d), F32),
                        pltpu.VMEM((hist_rows + tm, ck), F32), pltpu.VMEM((nck, hist_rows, ck), F32)],
        compiler_params=_cparams(("parallel", "arbitrary", "arbitrary")),
        name="ffn",
    )(x3, g_norm, hist, w_up, w_up, w_conv, b_conv, w_down, g_final)
    return y, hist_all[:, nt - 1]


def _sb_chains(zs, cbs, mo, mask):
    lsps, lons, firsts = [], [], []
    for chain in zs:
        for t, z in enumerate(chain):
            lon = jnp.maximum(z, 0.0) + jnp.log2(1.0 + jnp.exp2(-jnp.abs(z)))
            lsps.append(z - lon)
            lon = lon if (mask is None or t) else jnp.where(mask, lon, 0.0)
            firsts.append(lon[:, :1])
            lons.append(lon.astype(BF16))
    rs = [_dot(lon, mo) for lon in lons]
    out, new_cbs, n = [], [], 0
    for chain, cb in zip(zs, cbs):
        weights = []
        for t, z in enumerate(chain):
            tk = z.shape[1]
            a = jnp.exp2(lsps[n] + rs[n] + jnp.tile(cb, (1, tk // LANES)))
            weights.append((a if (mask is None or t) else jnp.where(mask, a, 0.0)).astype(BF16))
            cb = cb + jnp.broadcast_to(rs[n][:, :1] - firsts[n], cb.shape)
            n += 1
        out.append(weights)
        new_cbs.append(cb)
    return out, new_cbs


SB_TILES_PER_ITER = 4


def _sb_mo(tk):
    j = np.arange(tk)[:, None]
    s = np.arange(tk)[None, :]
    return jnp.asarray(-(j > s).astype(np.float32), BF16)


def _sb_prompt_kernel(bias_ref, q_ref, kt_ref, vt_ref, mo_ref, o_ref, acc_ref, cb_ref, *, tq):
    hp = pl.program_id(1)
    qi = pl.program_id(2)
    q = q_ref[0]
    lane = lax.broadcasted_iota(jnp.int32, q.shape, 1)
    tri = lax.broadcasted_iota(jnp.int32, (tq, tq), 1) < lax.broadcasted_iota(jnp.int32, (tq, tq), 0)
    mo = mo_ref[...]
    qh = [jnp.where(lane < SB_DIM, q, 0), jnp.where(lane >= SB_DIM, q, 0)]
    bias = [bias_ref[2 * hp], bias_ref[2 * hp + 1]]

    def tiles(kbs, mask, first):
        kts = [kt_ref[0, kb] for kb in kbs]
        zs = [[_dot(qh[sub], kt) + bias[sub] for kt in kts] for sub in range(2)]
        cbs = [jnp.zeros((tq, LANES), F32) if first else cb_ref[sub] for sub in range(2)]
        weights, cbs = _sb_chains(zs, cbs, mo, mask)
        vts = [vt_ref[0, kb] for kb in kbs]
        pvs = [[_dot_nt(a, vt) for a, vt in zip(weights[sub], vts)] for sub in range(2)]
        for sub in range(2):
            acc = sum(pvs[sub][1:], pvs[sub][0])
            acc_ref[sub] = acc if first else acc_ref[sub] + acc
            cb_ref[sub] = cbs[sub]

    n_first = qi % SB_TILES_PER_ITER + 1
    for size in range(1, SB_TILES_PER_ITER + 1):
        @pl.when(n_first == size)
        def _(size=size):
            tiles([qi - n for n in range(size)], tri, True)

    def body(i, carry):
        kb = qi - n_first - SB_TILES_PER_ITER * i
        tiles([kb - n for n in range(SB_TILES_PER_ITER)], None, False)
        return carry

    lax.fori_loop(0, qi // SB_TILES_PER_ITER, body, 0)
    o_ref[0] = jnp.where(lane < SB_DIM, acc_ref[0], acc_ref[1])


def _sb_prompt(q_b, kt_b, vt_b, bias2):
    b, s, hd = q_b.shape
    tq = kt_b.shape[3]
    nt = s // tq
    return pl.pallas_call(
        functools.partial(_sb_prompt_kernel, tq=tq),
        grid=(b, hd // LANES, nt),
        in_specs=[pl.BlockSpec(memory_space=pltpu.SMEM),
                  pl.BlockSpec((1, tq, LANES), lambda i, h, j: (i, j, h)),
                  pl.BlockSpec((1, nt, LANES, tq), lambda i, h, j: (i, 0, h, 0)),
                  pl.BlockSpec((1, nt, LANES, tq), lambda i, h, j: (i, 0, h, 0)),
                  pl.BlockSpec((tq, tq), lambda i, h, j: (0, 0))],
        out_specs=pl.BlockSpec((1, tq, LANES), lambda i, h, j: (i, j, h)),
        out_shape=jax.ShapeDtypeStruct((b, s, hd), F32),
        scratch_shapes=[pltpu.VMEM((2, tq, LANES), F32), pltpu.VMEM((2, tq, LANES), F32)],
        compiler_params=_cparams(("parallel", "parallel", "arbitrary")),
        name="sb_prompt",
    )(bias2, q_b, kt_b, vt_b, _sb_mo(tq))


SB_PAGES_PER_STEP = 8


def _sb_sample_kernel(pt_ref, qbd_ref, bias_ref, kn_ref, vn_ref, *rest, nq, npp):
    ck_refs, cv_refs = rest[:npp], rest[npp:2 * npp]
    mo_ref, o_ref, acc_ref, cb_ref = rest[2 * npp:]
    j = pl.program_id(1)
    qbd = qbd_ref[0]
    bias = bias_ref[...]
    mo = mo_ref[...]
    nh = SB_HEADS

    @pl.when(j == 0)
    def _():
        z = _dot_nt(qbd, kn_ref[0].astype(BF16)) + bias
        s_idx = lax.broadcasted_iota(jnp.int32, z.shape, 1)
        q_idx = lax.broadcasted_iota(jnp.int32, z.shape, 0) // nh
        w, cbs = _sb_chains([[z]], [jnp.zeros((z.shape[0], LANES), F32)], mo, s_idx < q_idx)
        acc_ref[...] = _dot(w[0][0], vn_ref[0].astype(BF16))
        cb_ref[...] = cbs[0]

    zs = [_dot(qbd, ck_refs[i][0, 0].astype(BF16)) + bias for i in range(npp)]
    w, cbs = _sb_chains([zs], [cb_ref[...]], mo, None)
    pvs = [_dot_nt(w[0][i], cv_refs[i][0, 0].astype(BF16)) for i in range(npp)]
    acc = acc_ref[...] + sum(pvs[1:], pvs[0])
    acc_ref[...] = acc
    cb_ref[...] = cbs[0]

    @pl.when(j == pl.num_programs(1) - 1)
    def _():
        r_h = lax.broadcasted_iota(jnp.int32, acc.shape, 0) % nh
        l_h = lax.broadcasted_iota(jnp.int32, acc.shape, 1) // SB_DIM
        sel = jnp.where(r_h == l_h, acc, 0.0)
        o_ref[0] = jnp.sum(sel.reshape(nq, nh, acc.shape[1]), axis=1)


def _sb_sample(q_b, k_new, v_new, cache_kt, cache_vt, layer, page_table, bias2):
    b, nq, hd = q_b.shape
    nh = SB_HEADS
    n_pages = page_table.shape[1]
    page = cache_kt.shape[3]
    npp = SB_PAGES_PER_STEP if n_pages % SB_PAGES_PER_STEP == 0 else 1
    qs = q_b.reshape(b, nq, nh, SB_DIM)
    eye = jnp.eye(nh, dtype=BF16)
    qbd = (qs[:, :, :, None, :] * eye[None, None, :, :, None]).reshape(b, nq * nh, hd)
    bias_b = jnp.broadcast_to(jnp.tile(bias2, nq)[:, None], (nq * nh, page))
    pad = lambda a: jnp.pad(a, ((0, 0), (0, page - nq), (0, 0)))
    page_spec = lambda i_page: pl.BlockSpec(
        (1, 1, hd, page), lambda i, j, pt: (layer, pt[i, n_pages - 1 - (j * npp + i_page)], 0, 0))
    return pl.pallas_call(
        functools.partial(_sb_sample_kernel, nq=nq, npp=npp),
        grid_spec=pltpu.PrefetchScalarGridSpec(
            num_scalar_prefetch=1,
            grid=(b, n_pages // npp),
            in_specs=([pl.BlockSpec((1, nq * nh, hd), lambda i, j, pt: (i, 0, 0)),
                       pl.BlockSpec((nq * nh, page), lambda i, j, pt: (0, 0)),
                       pl.BlockSpec((1, page, hd), lambda i, j, pt: (i, 0, 0)),
                       pl.BlockSpec((1, page, hd), lambda i, j, pt: (i, 0, 0))]
                      + [page_spec(i) for i in range(npp)] + [page_spec(i) for i in range(npp)]
                      + [pl.BlockSpec((page, page), lambda i, j, pt: (0, 0))]),
            out_specs=pl.BlockSpec((1, nq, hd), lambda i, j, pt: (i, 0, 0)),
            scratch_shapes=[pltpu.VMEM((nq * nh, hd), F32), pltpu.VMEM((nq * nh, LANES), F32)],
        ),
        out_shape=jax.ShapeDtypeStruct((b, nq, hd), F32),
        compiler_params=_cparams(("parallel", "arbitrary")),
        name="sb_sample",
    )(page_table, qbd, bias_b, pad(k_new), pad(v_new), *([cache_kt] * npp), *([cache_vt] * npp), _sb_mo(page))


def _even_layer(x3, g_mix, w_in_p, s5_ops, d_skip, w_glu, b_glu, w_gk_p, b_gk, gla_norm, w_top, w_bot,
                s5_h0, gla_s0, chunk):
    b, s, d = x3.shape
    s5_w = d_skip.shape[1]
    ng = s5_w // S5_GROUP
    x2 = x3.reshape(b * s, d)
    p = _inproj(x2, g_mix, w_in_p)
    p3 = p.reshape(b, s, p.shape[1])
    nblk, _, pb = s5_ops[1].shape
    if s5_h0 is None:
        h0r = h0i = jnp.zeros((b, nblk, pb), F32)
    else:
        h0r, h0i = s5_h0[0].reshape(b, nblk, pb), s5_h0[1].reshape(b, nblk, pb)
    y_conv, hre, him = _s5(p3, s5_ops, h0r, h0i, chunk)
    y_conv = y_conv.reshape(b * s, s5_w)
    fin = lambda a: a.reshape(b, ng, pb * nblk // ng)
    gla_tc = _tile(s, 128) if s >= 16 else 16
    gla_valid = min(s, gla_tc)
    sp = p3 if s >= gla_tc else jnp.pad(p3, ((0, 0), (0, gla_tc - s), (0, 0)))
    s0 = (jnp.zeros((b, GLA_HEADS // 2, 2 * GLA_DK, GLA_DV), F32) if gla_s0 is None
          else gla_s0.reshape(b, GLA_HEADS // 2, 2 * GLA_DK, GLA_DV))
    o_gla, s_fin = _gla(sp, w_gk_p, b_gk, gla_norm, s0, gla_tc, gla_valid)
    o_gla = o_gla[:, :s].reshape(b * s, o_gla.shape[2])
    x2 = _mix_out(x2, y_conv, p, o_gla, d_skip, w_glu, b_glu, w_top, w_bot)
    return x2.reshape(b, s, d), fin(hre), fin(him), s_fin.reshape(b, GLA_HEADS, GLA_DK, GLA_DV)


def kernel(x_prompt, x_sample, state_s5_re, state_s5_im, state_gla, cache_k, cache_v, state_ffn_conv, page_table, norm_mix, norm_ffn, norm_final, w_in0, s5_a_re, s5_a_im, s5_log_dt, s5_b_re, s5_b_im, s5_c_re, s5_c_im, s5_d, s5_w_glu, s5_b_glu, gla_w_gk, gla_b_gk, gla_norm, w_out0, w_qkv1, w_out1, sb_bias, ffn_w_up, ffn_w_conv, ffn_b_conv, ffn_w_down):
    bp, sp, d = x_prompt.shape
    bs, ss, _ = x_sample.shape
    depth = norm_mix.shape[0]
    dff = ffn_w_down.shape[1]
    s5_w = s5_d.shape[1]
    xp, xs = x_prompt, x_sample
    outs_p = {k: [] for k in ("s5r", "s5i", "gla", "k", "v", "conv")}
    outs_s = {k: [] for k in ("s5r", "s5i", "gla", "k", "v", "conv")}
    hist_rows_p = 8
    for l in range(depth):
        i = l // 2
        g_mix = norm_mix[l][None, :]
        if l % 2 == 0:
            in0 = w_in0.shape[2]
            in0p = -(-in0 // LANES) * LANES
            w_in_p = jnp.pad(w_in0[i], ((0, 0), (0, in0p - in0))).astype(BF16)
            w_gk_p = jnp.pad(gla_w_gk[i], ((0, LANES - GLA_RANK), (0, 0))).astype(BF16)
            common = (s5_d[i][None, :], s5_w_glu[i].astype(BF16), s5_b_glu[i][None, :], w_gk_p,
                      gla_b_gk[i][None, :], gla_norm[i][None, :], w_out0[i][:s5_w].astype(BF16),
                      w_out0[i][s5_w:].astype(BF16))
            s5_args = (s5_a_re[i], s5_a_im[i], s5_log_dt[i], s5_b_re[i], s5_b_im[i], s5_c_re[i], s5_c_im[i])
            chunk_p = _tile(sp, S5_CHUNK)
            ops_p = _s5_operators(*s5_args, chunk_p, sp // chunk_p)
            ops_s = _s5_operators(*s5_args, ss, 1)
            xp, hr, hi, sg = _even_layer(xp, g_mix, w_in_p, ops_p, *common, None, None, chunk_p)
            xs, hr2, hi2, sg2 = _even_layer(xs, g_mix, w_in_p, ops_s, *common,
                                            (state_s5_re[i], state_s5_im[i]), state_gla[i], ss)
            for o, vals in ((outs_p, (hr, hi, sg)), (outs_s, (hr2, hi2, sg2))):
                o["s5r"].append(vals[0])
                o["s5i"].append(vals[1])
                o["gla"].append(vals[2])
        else:
            w_qkv = w_qkv1[i].astype(BF16)
            w_out = w_out1[i].astype(BF16)
            bias2 = sb_bias[i] * LOG2E
            qb, kt, vt, ktb, vtb = _qkv_t(xp, g_mix, w_qkv[:, :d], w_qkv[:, d:].T, _tile(sp, 256))
            att = _sb_prompt(qb, ktb, vtb, bias2)
            xp = _proj_res(xp.reshape(bp * sp, d), att.reshape(bp * sp, d), w_out).reshape(bp, sp, d)
            qsb, ks, vs, _, _ = _qkv(xs.reshape(bs * ss, d), g_mix, w_qkv)
            as_t = lambda c: c.transpose(0, 1, 3, 4, 2).reshape(c.shape[0], c.shape[1], d, c.shape[2])
            att_s = _sb_sample(qsb.reshape(bs, ss, d), ks.reshape(bs, ss, d), vs.reshape(bs, ss, d),
                               as_t(cache_k), as_t(cache_v), i, page_table, bias2)
            xs = _proj_res(xs.reshape(bs * ss, d), att_s.reshape(bs * ss, d), w_out).reshape(bs, ss, d)
            outs_p["k"].append(kt.reshape(bp, SB_HEADS, SB_DIM, sp).transpose(0, 3, 1, 2))
            outs_p["v"].append(vt.reshape(bp, SB_HEADS, SB_DIM, sp).transpose(0, 3, 1, 2))
            outs_s["k"].append(ks.reshape(bs, ss, SB_HEADS, SB_DIM))
            outs_s["v"].append(vs.reshape(bs, ss, SB_HEADS, SB_DIM))
        final = l == depth - 1
        w_up = ffn_w_up[l].astype(BF16)
        w_down = ffn_w_down[l].astype(BF16)
        ffn_args = (w_up, ffn_w_conv[l], ffn_b_conv[l][None, :], w_down, norm_final[None, :])
        xp, hist_p = _ffn(xp, norm_ffn[l][None, :], jnp.zeros((bp, hist_rows_p, dff), F32), *ffn_args, 1, final)
        outs_p["conv"].append(hist_p[:, hist_rows_p - (CONV_W - 1):])
        hist_s = state_ffn_conv[l].transpose(1, 0, 2).reshape(1, (CONV_W - 1) * bs, dff)
        xs_t, hist_s = _ffn(xs.transpose(1, 0, 2).reshape(1, ss * bs, d), norm_ffn[l][None, :], hist_s,
                            *ffn_args, bs, final)
        xs = xs_t.reshape(ss, bs, d).transpose(1, 0, 2)
        outs_s["conv"].append(hist_s.reshape(CONV_W - 1, bs, dff).transpose(1, 0, 2))
    st = jnp.stack
    return (xp, xs,
            st(outs_p["s5r"]), st(outs_p["s5i"]), st(outs_p["gla"]), st(outs_p["k"]), st(outs_p["v"]), st(outs_p["conv"]),
            st(outs_s["s5r"]), st(outs_s["s5i"]), st(outs_s["gla"]), st(outs_s["k"]), st(outs_s["v"]), st(outs_s["conv"]))
```

```python
import functools
import math

import jax
import jax.numpy as jnp
import numpy as np
from jax import lax
from jax.experimental import pallas as pl
from jax.experimental.pallas import tpu as pltpu

F32 = jnp.float32
BF16 = jnp.bfloat16
EPS = 1e-6

S5_GROUP = 16
S5_CHUNK = 16
GLA_HEADS = 4
GLA_DK = 64
GLA_DV = 128
GLA_RANK = 16
GLA_GATE_NORM = 16.0
SB_HEADS = 16
SB_DIM = 64
CONV_W = 3
LOG2E = math.log2(math.e)
SB_QSCALE = SB_DIM ** -0.5 * LOG2E
LANES = 128
VMEM_LIMIT = 56 * 1024 * 1024


def _cparams(sem):
    return pltpu.CompilerParams(dimension_semantics=sem, vmem_limit_bytes=VMEM_LIMIT)


def _tile(n, pref):
    if n <= pref:
        return n
    t = pref
    while n % t:
        t //= 2
    return t


def _rmsnorm(x, g):
    y = x * lax.rsqrt(jnp.mean(x * x, axis=-1, keepdims=True) + EPS)
    return y * g


def _split2(x):
    hi = x.astype(BF16)
    lo = (x - hi.astype(F32)).astype(BF16)
    return hi, lo


def _dot(a, b):
    return jnp.dot(a, b, preferred_element_type=F32)


def _dot_nt(a, b):
    return lax.dot_general(a, b, (((1,), (1,)), ((), ())), preferred_element_type=F32)


def _dot_tn(a, b):
    return lax.dot_general(a, b, (((0,), (0,)), ((), ())), preferred_element_type=F32)


def _log_sigmoid_pair(z):
    lsp = jnp.minimum(z, 0.0) - jnp.log1p(jnp.exp(-jnp.abs(z)))
    return lsp, lsp - z


def _inproj_kernel(x_ref, g_ref, w_ref, o_ref):
    h = _rmsnorm(x_ref[...], g_ref[...]).astype(BF16)
    o_ref[...] = _dot(h, w_ref[...])


def _inproj(x, g, w):
    t, d = x.shape
    n = w.shape[1]
    tm = _tile(t, 512)
    return pl.pallas_call(
        _inproj_kernel,
        grid=(t // tm,),
        in_specs=[pl.BlockSpec((tm, d), lambda i: (i, 0)),
                  pl.BlockSpec((1, d), lambda i: (0, 0)),
                  pl.BlockSpec((d, n), lambda i: (0, 0))],
        out_specs=pl.BlockSpec((tm, n), lambda i: (i, 0)),
        out_shape=jax.ShapeDtypeStruct((t, n), F32),
        compiler_params=_cparams(("parallel",)),
        name="inproj",
    )(x, g, w)


def _qkv_kernel(x_ref, g_ref, w_ref, qb_ref, k_ref, v_ref, kb_ref, vb_ref):
    d = x_ref.shape[1]
    h = _rmsnorm(x_ref[...], g_ref[...]).astype(BF16)
    p = _dot(h, w_ref[...])
    q, k, v = p[:, :d], p[:, d:2 * d], p[:, 2 * d:]
    qb_ref[...] = (q * SB_QSCALE).astype(BF16)
    k_ref[...] = k
    v_ref[...] = v
    kb_ref[...] = k.astype(BF16)
    vb_ref[...] = v.astype(BF16)


def _qkv(x, g, w):
    t, d = x.shape
    tm = _tile(t, 256)
    row = pl.BlockSpec((tm, d), lambda i: (i, 0))
    return pl.pallas_call(
        _qkv_kernel,
        grid=(t // tm,),
        in_specs=[row, pl.BlockSpec((1, d), lambda i: (0, 0)), pl.BlockSpec((d, 3 * d), lambda i: (0, 0))],
        out_specs=[row, row, row, row, row],
        out_shape=[jax.ShapeDtypeStruct((t, d), BF16), jax.ShapeDtypeStruct((t, d), F32),
                   jax.ShapeDtypeStruct((t, d), F32), jax.ShapeDtypeStruct((t, d), BF16),
                   jax.ShapeDtypeStruct((t, d), BF16)],
        compiler_params=_cparams(("parallel",)),
        name="qkv",
    )(x, g, w)


def _qkv_t_kernel(x_ref, g_ref, wq_ref, wkvt_ref, qb_ref, kt_ref, vt_ref, ktb_ref, vtb_ref):
    d = x_ref.shape[2]
    h = _rmsnorm(x_ref[0], g_ref[...]).astype(BF16)
    qb_ref[0] = (_dot(h, wq_ref[...]) * SB_QSCALE).astype(BF16)
    kvt = _dot_nt(wkvt_ref[...], h)
    kt_ref[0] = kvt[:d]
    vt_ref[0] = kvt[d:]
    ktb_ref[0, 0] = kvt[:d].astype(BF16)
    vtb_ref[0, 0] = kvt[d:].astype(BF16)


def _qkv_t(x3, g, wq, wkvt, tm):
    b, s, d = x3.shape
    nt = s // tm
    return pl.pallas_call(
        _qkv_t_kernel,
        grid=(b, nt),
        in_specs=[pl.BlockSpec((1, tm, d), lambda i, t: (i, t, 0)),
                  pl.BlockSpec((1, d), lambda i, t: (0, 0)),
                  pl.BlockSpec((d, d), lambda i, t: (0, 0)),
                  pl.BlockSpec((2 * d, d), lambda i, t: (0, 0))],
        out_specs=[pl.BlockSpec((1, tm, d), lambda i, t: (i, t, 0)),
                   pl.BlockSpec((1, d, tm), lambda i, t: (i, 0, t)),
                   pl.BlockSpec((1, d, tm), lambda i, t: (i, 0, t)),
                   pl.BlockSpec((1, 1, d, tm), lambda i, t: (i, t, 0, 0)),
                   pl.BlockSpec((1, 1, d, tm), lambda i, t: (i, t, 0, 0))],
        out_shape=[jax.ShapeDtypeStruct((b, s, d), BF16), jax.ShapeDtypeStruct((b, d, s), F32),
                   jax.ShapeDtypeStruct((b, d, s), F32), jax.ShapeDtypeStruct((b, nt, d, tm), BF16),
                   jax.ShapeDtypeStruct((b, nt, d, tm), BF16)],
        compiler_params=_cparams(("parallel", "parallel")),
        name="qkv_t",
    )(x3, g, wq, wkvt)


def _proj_res_kernel(x_ref, a_ref, w_ref, o_ref):
    o_ref[...] = x_ref[...] + _dot(a_ref[...].astype(BF16), w_ref[...])


def _proj_res(x, a, w):
    t, d = x.shape
    k = a.shape[1]
    tm = _tile(t, 512)
    return pl.pallas_call(
        _proj_res_kernel,
        grid=(t // tm,),
        in_specs=[pl.BlockSpec((tm, d), lambda i: (i, 0)),
                  pl.BlockSpec((tm, k), lambda i: (i, 0)),
                  pl.BlockSpec((k, d), lambda i: (0, 0))],
        out_specs=pl.BlockSpec((tm, d), lambda i: (i, 0)),
        out_shape=jax.ShapeDtypeStruct((t, d), F32),
        compiler_params=_cparams(("parallel",)),
        name="proj_res",
    )(x, a, w)


S5_BLOCK_GROUPS = LANES // S5_GROUP


def _s5_operators(a_re, a_im, log_dt, b_re, b_im, c_re, c_im, chunk, n_chunks):
    hp = lax.Precision.HIGHEST
    g, p = a_re.shape
    dt = jnp.exp(log_dt)[:, None]
    mag = jnp.exp(a_re * dt)
    ab_re, ab_im = mag * jnp.cos(a_im * dt), mag * jnp.sin(a_im * dt)
    den = a_re * a_re + a_im * a_im
    nr, ni = ab_re - 1.0, ab_im
    coef_re = (nr * a_re + ni * a_im) / den
    coef_im = (ni * a_re - nr * a_im) / den
    bb_re = coef_re[..., None] * b_re - coef_im[..., None] * b_im
    bb_im = coef_re[..., None] * b_im + coef_im[..., None] * b_re
    pr, pi = [jnp.ones_like(ab_re)], [jnp.zeros_like(ab_re)]
    for _ in range(chunk):
        r0, i0 = pr[-1], pi[-1]
        pr.append(r0 * ab_re - i0 * ab_im)
        pi.append(r0 * ab_im + i0 * ab_re)
    pw_re, pw_im = jnp.stack(pr, 1), jnp.stack(pi, 1)
    ca_re = c_re[:, None] * pw_re[:, :, None, :] - c_im[:, None] * pw_im[:, :, None, :]
    ca_im = c_re[:, None] * pw_im[:, :, None, :] + c_im[:, None] * pw_re[:, :, None, :]
    taps = (jnp.einsum('gtcp,gpd->gtcd', ca_re[:, :chunk], bb_re, precision=hp)
            - jnp.einsum('gtcp,gpd->gtcd', ca_im[:, :chunk], bb_im, precision=hp))
    gb = S5_BLOCK_GROUPS
    nb = g // gb
    lb = chunk * LANES

    def group_diag(x):
        n, _, r, w = x.shape
        tiled = jnp.dot(x.reshape(n * gb * r, w), np.tile(np.eye(w, dtype=np.float32), (1, gb)), precision=hp)
        same = (np.arange(gb * r)[:, None] // r) == (np.arange(gb * w)[None, :] // w)
        return jnp.where(same[None], tiled.reshape(n, gb * r, gb * w), 0.0).astype(BF16)

    taps_rev = taps[:, np.arange(chunk - 1, -1, -1)].reshape(nb, gb, chunk, S5_GROUP, S5_GROUP)
    t_b = group_diag(taps_rev.transpose(0, 2, 1, 4, 3).reshape(nb * chunk, gb, S5_GROUP, S5_GROUP)).reshape(nb, lb, LANES)
    t_a = jnp.concatenate([t_b[:, LANES:], jnp.zeros((nb, LANES, LANES), BF16)], axis=1)
    m = jnp.concatenate([t_a, t_b], axis=2)
    rev_re, rev_im = jnp.stack(pr[chunk - 1::-1], 1), jnp.stack(pi[chunk - 1::-1], 1)
    bt_re, bt_im = bb_re.transpose(0, 2, 1)[:, None], bb_im.transpose(0, 2, 1)[:, None]
    w_re = rev_re[:, :, None, :] * bt_re - rev_im[:, :, None, :] * bt_im
    w_im = rev_re[:, :, None, :] * bt_im + rev_im[:, :, None, :] * bt_re
    blockw = lambda w: group_diag(w.reshape(nb, gb, chunk, S5_GROUP, p).transpose(0, 2, 1, 3, 4)
                                  .reshape(nb * chunk, gb, S5_GROUP, p)).reshape(nb, lb, gb * p)
    blockv = lambda v: (group_diag(v.reshape(nb, gb, chunk, S5_GROUP, p).transpose(0, 2, 1, 4, 3)
                                   .reshape(nb * chunk, gb, p, S5_GROUP))
                        .reshape(nb, chunk, gb * p, LANES).transpose(0, 2, 1, 3).reshape(nb, gb * p, lb))
    sr, si = [pw_re[:, chunk]], [pw_im[:, chunk]]
    n_steps = max(1, int(math.log2(n_chunks))) if n_chunks > 1 else 1
    for _ in range(n_steps - 1):
        r0, i0 = sr[-1], si[-1]
        sr.append(r0 * r0 - i0 * i0)
        si.append(2.0 * r0 * i0)
    blockp = lambda x: jnp.stack(x, 1).reshape(nb, gb, len(x), p).transpose(0, 2, 1, 3).reshape(nb, len(x), gb * p)
    return (m, blockw(w_re), blockw(w_im), blockv(ca_re[:, 1:]), blockv(-ca_im[:, 1:]), blockp(sr), blockp(si))


def _s5_kernel(u_ref, m_ref, wre_ref, wim_ref, vre_ref, vim_ref, are_ref, aim_ref, h0re_ref, h0im_ref,
               y_ref, hre_ref, him_ref, *, chunk, n_chunks):
    if n_chunks == 1:
        xs = [u_ref[:, i, :] for i in range(chunk)]
    else:
        xs = [u_ref[0, pl.ds(i, n_chunks, stride=chunk), :] for i in range(chunk)]
    z_hi, z_lo = _split2(jnp.concatenate(xs, axis=1))
    lb = chunk * LANES
    y_pairs = [_dot(z_hi[:, :2 * LANES * (k + 1)], m_ref[0, lb - 2 * LANES * (k + 1):lb, :]) for k in range(chunk // 2)]
    s_re = _dot(z_hi, wre_ref[0]) + _dot(z_lo, wre_ref[0])
    s_im = _dot(z_hi, wim_ref[0]) + _dot(z_lo, wim_ref[0])
    pw_re, pw_im = are_ref[0], aim_ref[0]
    h0r, h0i = h0re_ref[:, 0, :], h0im_ref[:, 0, :]
    ar, ai = pw_re[0:1], pw_im[0:1]
    inj_re, inj_im = ar * h0r - ai * h0i, ar * h0i + ai * h0r
    if n_chunks == 1:
        s_re, s_im = s_re + inj_re, s_im + inj_im
        hs_re, hs_im = h0r, h0i
    else:
        row = lax.broadcasted_iota(jnp.int32, s_re.shape, 0)
        first = row == 0
        s_re = s_re + jnp.where(first, inj_re, 0.0)
        s_im = s_im + jnp.where(first, inj_im, 0.0)
        shift, j = 1, 0
        while shift < n_chunks:
            ar, ai = pw_re[j:j + 1], pw_im[j:j + 1]
            keep = row >= shift
            pr = jnp.where(keep, pltpu.roll(s_re, shift, 0), 0.0)
            pi = jnp.where(keep, pltpu.roll(s_im, shift, 0), 0.0)
            s_re, s_im = s_re + ar * pr - ai * pi, s_im + ar * pi + ai * pr
            shift, j = shift * 2, j + 1
        hs_re = jnp.where(first, h0r, pltpu.roll(s_re, 1, 0))
        hs_im = jnp.where(first, h0i, pltpu.roll(s_im, 1, 0))
    y = _dot(hs_re.astype(BF16), vre_ref[0]) + _dot(hs_im.astype(BF16), vim_ref[0])
    for i in range(chunk):
        yi = y[:, i * LANES:(i + 1) * LANES] + y_pairs[i // 2][:, (i % 2) * LANES:(i % 2 + 1) * LANES]
        if n_chunks == 1:
            y_ref[:, i, :] = yi
        else:
            y_ref[0, pl.ds(i, n_chunks, stride=chunk), :] = yi
    if n_chunks == 1:
        hre_ref[:, 0, :] = s_re
        him_ref[:, 0, :] = s_im
    else:
        hre_ref[0] = s_re[n_chunks - 1:n_chunks]
        him_ref[0] = s_im[n_chunks - 1:n_chunks]


def _s5(p3, ops, h0_re, h0_im, chunk):
    b, s, _ = p3.shape
    m, w_re, w_im, v_re, v_im, a_re, a_im = ops
    nblk, lb, pb = w_re.shape
    nc = s // chunk
    ns = a_re.shape[1]
    bb = b if nc == 1 else 1
    h0_re = h0_re.reshape(b, nblk, 1, pb)
    h0_im = h0_im.reshape(b, nblk, 1, pb)
    blk = lambda *shape: pl.BlockSpec((1,) + shape, lambda k, i: (k, 0, 0))
    st = pl.BlockSpec((bb, None, 1, pb), lambda k, i: (i, k, 0, 0))
    y, hre, him = pl.pallas_call(
        functools.partial(_s5_kernel, chunk=chunk, n_chunks=nc),
        grid=(nblk, b // bb),
        in_specs=[pl.BlockSpec((bb, s, LANES), lambda k, i: (i, 0, k)),
                  blk(lb, 2 * LANES), blk(lb, pb), blk(lb, pb), blk(pb, lb), blk(pb, lb), blk(ns, pb), blk(ns, pb),
                  st, st],
        out_specs=[pl.BlockSpec((bb, s, LANES), lambda k, i: (i, 0, k)), st, st],
        out_shape=[jax.ShapeDtypeStruct((b, s, nblk * LANES), F32), jax.ShapeDtypeStruct((b, nblk, 1, pb), F32),
                   jax.ShapeDtypeStruct((b, nblk, 1, pb), F32)],
        compiler_params=_cparams(("parallel", "arbitrary")),
        name="s5",
    )(p3, m, w_re, w_im, v_re, v_im, a_re, a_im, h0_re, h0_im)
    return y, hre.reshape(b, nblk, pb), him.reshape(b, nblk, pb)


def _gla_constants(tc):
    nl = int(math.log2(tc))
    t = np.arange(tc)
    blocks = []
    for j in range(nl):
        m = 1 << j
        base = t & ~(2 * m - 1)
        ref = base + m - 1
        col = np.arange(tc)[None, :]
        is_q = ((t >> j) & 1) == 1
        dq = (col > ref[:, None]) & (col <= t[:, None])
        dk = (col > t[:, None]) & (col <= ref[:, None])
        blocks.append(np.where(is_q[:, None], dq, dk))
    blocks.append(np.tril(np.ones((tc, tc), bool)))
    blocks.append(np.triu(np.ones((tc, tc), bool), 1))
    dall = np.concatenate(blocks, 0).astype(np.float32)
    x = t[:, None] ^ t[None, :]
    lvl = np.where(x > 0, np.floor(np.log2(np.maximum(x, 1))).astype(np.int32), -1)
    lvl = np.where(t[:, None] > t[None, :], lvl, np.where(t[:, None] == t[None, :], -1, -2)).astype(np.int32)
    return jnp.asarray(dall, BF16), jnp.asarray(lvl)


def _gla_kernel(q_ref, k_ref, v_ref, g_ref, r_ref, wgk_ref, bgk_ref, gn_ref, s0_ref, dall_ref, lvl_ref,
                o_ref, sfin_ref, st_ref, *, tc, valid):
    c = pl.program_id(1)
    nl = int(math.log2(tc))
    hdk = GLA_HEADS * GLA_DK

    @pl.when(c == 0)
    def _():
        st_ref[...] = s0_ref[0]

    q = q_ref[0] * (GLA_DK ** -0.5)
    k = k_ref[0]
    v = v_ref[0]
    x = _dot(r_ref[0].astype(BF16), wgk_ref[...]) + bgk_ref[...]
    gk = _log_sigmoid_pair(x)[0] / GLA_GATE_NORM
    if valid < tc:
        live = lax.broadcasted_iota(jnp.int32, gk.shape, 0) < valid
        gk = jnp.where(live, gk, 0.0)
        k = jnp.where(live, k, 0.0)
    g2 = jnp.concatenate(_split2(gk), axis=1)
    r2 = _dot(dall_ref[...], g2)
    ex = jnp.exp(r2[:, :hdk] + r2[:, hdk:])
    e_cum = ex[nl * tc:(nl + 1) * tc]
    e_rem = ex[(nl + 1) * tc:]
    bl = _dot_tn(g2, jnp.ones((tc, GLA_DV), BF16))
    decay = jnp.exp(bl[:hdk] + bl[hdk:])

    row = lax.broadcasted_iota(jnp.int32, q.shape, 0)
    lane = lax.broadcasted_iota(jnp.int32, (tc, 2 * GLA_DK), 1)
    lvl = lvl_ref[...]
    q_in = (q * e_cum).astype(BF16)
    k_out = (k * e_rem).astype(BF16)
    xs = []
    for j in range(nl):
        is_q = ((row >> j) & 1) == 1
        xs.append((jnp.where(is_q, q, k) * ex[j * tc:(j + 1) * tc]).astype(BF16))
    qb, kb = q.astype(BF16), k.astype(BF16)
    gn = gn_ref[...]
    heads = range(GLA_HEADS)
    ps = [slice((h // 2) * 2 * GLA_DK, (h // 2 + 1) * 2 * GLA_DK) for h in heads]
    in_head = [(lane < GLA_DK) if h % 2 == 0 else (lane >= GLA_DK) for h in heads]
    g_diag = [_dot_nt(jnp.where(in_head[h], qb[:, ps[h]], 0), kb[:, ps[h]]) for h in heads]
    g_lvl = [[_dot_nt(jnp.where(in_head[h], xs[j][:, ps[h]], 0), xs[j][:, ps[h]]) for h in heads] for j in range(nl)]
    vh = [v[:, h * GLA_DV:(h + 1) * GLA_DV].astype(BF16) for h in heads]
    st = [st_ref[pair] for pair in range(GLA_HEADS // 2)]
    o_inter = [_dot(jnp.where(in_head[h], q_in[:, ps[h]], 0), st[h // 2].astype(BF16)) for h in heads]
    upd = [_dot_tn(k_out[:, ps[h]], vh[h]) for h in heads]
    att = []
    for h in heads:
        a = jnp.where(lvl == -1, g_diag[h], 0.0)
        for j in range(nl):
            a = jnp.where(lvl == j, g_lvl[j][h], a)
        att.append(a.astype(BF16))
    o_intra = [_dot(att[h], vh[h]) for h in heads]
    rsel = lax.broadcasted_iota(jnp.int32, (2 * GLA_DK, GLA_DV), 0)
    for pair in range(GLA_HEADS // 2):
        new = jnp.where(rsel < GLA_DK, upd[2 * pair], upd[2 * pair + 1])
        st_ref[pair] = decay[ps[2 * pair]] * st[pair] + new
    for h in heads:
        o = o_intra[h] + o_inter[h]
        o = o * lax.rsqrt(jnp.mean(o * o, axis=-1, keepdims=True) + EPS) * gn
        gh = g_ref[0, :, h * GLA_DV:(h + 1) * GLA_DV]
        o_ref[0, :, h * GLA_DV:(h + 1) * GLA_DV] = o * (gh * jax.nn.sigmoid(gh))

    sfin_ref[0] = st_ref[...]


def _gla(p3, w_gk, b_gk, gla_norm, s0, tc, valid):
    b, s, _ = p3.shape
    hdk, hdv = GLA_HEADS * GLA_DK, GLA_HEADS * GLA_DV
    u_w = 2 * hdk
    dall, lvl = _gla_constants(tc)
    const = lambda shape: pl.BlockSpec(shape, lambda i, j: tuple(0 for _ in shape))
    return pl.pallas_call(
        functools.partial(_gla_kernel, tc=tc, valid=valid),
        grid=(b, s // tc),
        in_specs=[pl.BlockSpec((1, tc, hdk), lambda i, j: (i, j, u_w // hdk)),
                  pl.BlockSpec((1, tc, hdk), lambda i, j: (i, j, u_w // hdk + 1)),
                  pl.BlockSpec((1, tc, hdv), lambda i, j: (i, j, (u_w + 2 * hdk) // hdv)),
                  pl.BlockSpec((1, tc, hdv), lambda i, j: (i, j, (u_w + 2 * hdk) // hdv + 1)),
                  pl.BlockSpec((1, tc, LANES), lambda i, j: (i, j, (u_w + 2 * hdk + 2 * hdv) // LANES)),
                  const((LANES, hdk)), const((1, hdk)), const((1, GLA_DV)),
                  pl.BlockSpec((1, GLA_HEADS // 2, 2 * GLA_DK, GLA_DV), lambda i, j: (i, 0, 0, 0)),
                  const(dall.shape), const(lvl.shape)],
        out_specs=[pl.BlockSpec((1, tc, hdv), lambda i, j: (i, j, 0)),
                   pl.BlockSpec((1, GLA_HEADS // 2, 2 * GLA_DK, GLA_DV), lambda i, j: (i, 0, 0, 0))],
        out_shape=[jax.ShapeDtypeStruct((b, s, hdv), F32),
                   jax.ShapeDtypeStruct((b, GLA_HEADS // 2, 2 * GLA_DK, GLA_DV), F32)],
        scratch_shapes=[pltpu.VMEM((GLA_HEADS // 2, 2 * GLA_DK, GLA_DV), F32)],
        compiler_params=_cparams(("parallel", "arbitrary")),
        name="gla",
    )(p3, p3, p3, p3, p3, w_gk, b_gk, gla_norm, s0, dall, lvl)


def _mix_out_kernel(x_ref, yc_ref, u_ref, og_ref, d_ref, wglu_ref, bglu_ref, wtop_ref, wbot_ref, o_ref):
    y = yc_ref[...] + d_ref[...] * u_ref[...]
    z = jax.nn.gelu(y)
    s5 = z * jax.nn.sigmoid(_dot(z.astype(BF16), wglu_ref[...]) + bglu_ref[...])
    o_ref[...] = (x_ref[...] + _dot(s5.astype(BF16), wtop_ref[...])
                  + _dot(og_ref[...].astype(BF16), wbot_ref[...]))


def _mix_out(x, y_conv, p, o_gla, d_skip, w_glu, b_glu, w_top, w_bot):
    t, d = x.shape
    w5 = y_conv.shape[1]
    tm = _tile(t, 512)
    row = lambda w: pl.BlockSpec((tm, w), lambda i: (i, 0))
    const = lambda a, b: pl.BlockSpec((a, b), lambda i: (0, 0))
    return pl.pallas_call(
        _mix_out_kernel,
        grid=(t // tm,),
        in_specs=[row(d), row(w5), row(w5), row(o_gla.shape[1]), const(1, w5), const(w5, w5), const(1, w5),
                  const(w5, d), const(o_gla.shape[1], d)],
        out_specs=row(d),
        out_shape=jax.ShapeDtypeStruct((t, d), F32),
        compiler_params=_cparams(("parallel",)),
        name="mix_out",
    )(x, y_conv, p, o_gla, d_skip, w_glu, b_glu, w_top, w_bot)


def _ffn_kernel(x_ref, gn_ref, hist_ref, wu_ref, wg_ref, wc_ref, bc_ref, wd_ref, gf_ref,
                o_ref, hist_out_ref, h_scr, acc_scr, pad_scr, carry_scr, *, rs, hist_rows, final_norm):
    t = pl.program_id(1)
    c = pl.program_id(2)
    tm = x_ref.shape[1]

    @pl.when(c == 0)
    def _():
        h_scr[...] = _rmsnorm(x_ref[0], gn_ref[...]).astype(BF16)
        acc_scr[...] = jnp.zeros_like(acc_scr)

    @pl.when(t == 0)
    def _():
        carry_scr[c] = hist_ref[0]

    h = h_scr[...]
    u = _dot(h, wu_ref[...])
    g = _dot(h, wg_ref[...])
    pad_scr[0:hist_rows] = carry_scr[c]
    pad_scr[hist_rows:] = u
    wc = wc_ref[...]
    uc = bc_ref[...]
    for j in range(CONV_W):
        off = hist_rows - (CONV_W - 1 - j) * rs
        uc = uc + wc[j:j + 1] * pad_scr[off:off + tm]
    new_hist = pad_scr[tm:tm + hist_rows]
    carry_scr[c] = new_hist
    hist_out_ref[0, 0] = new_hist
    hdn = jax.nn.gelu(uc) * g
    acc_scr[...] += _dot(hdn.astype(BF16), wd_ref[...])

    @pl.when(c == pl.num_programs(2) - 1)
    def _():
        y = x_ref[0] + acc_scr[...]
        if final_norm:
            y = _rmsnorm(y, gf_ref[...])
        o_ref[0] = y


def _ffn(x3, g_norm, hist, w_up, w_conv, b_conv, w_down, g_final, rs, final_norm):
    nb, s, d = x3.shape
    dff = w_down.shape[0]
    hist_rows = hist.shape[1]
    tm = _tile(s, 512)
    nt = s // tm
    ck = dff // 2 if (dff // 2) % LANES == 0 else dff
    nck = dff // ck
    y, hist_all = pl.pallas_call(
        functools.partial(_ffn_kernel, rs=rs, hist_rows=hist_rows, final_norm=final_norm),
        grid=(nb, s // tm, nck),
        in_specs=[pl.BlockSpec((1, tm, d), lambda b, t, c: (b, t, 0)),
                  pl.BlockSpec((1, d), lambda b, t, c: (0, 0)),
                  pl.BlockSpec((1, hist_rows, ck), lambda b, t, c: (b, 0, c)),
                  pl.BlockSpec((d, ck), lambda b, t, c: (0, c)),
                  pl.BlockSpec((d, ck), lambda b, t, c: (0, nck + c)),
                  pl.BlockSpec((CONV_W, ck), lambda b, t, c: (0, c)),
                  pl.BlockSpec((1, ck), lambda b, t, c: (0, c)),
                  pl.BlockSpec((ck, d), lambda b, t, c: (c, 0)),
                  pl.BlockSpec((1, d), lambda b, t, c: (0, 0))],
        out_specs=[pl.BlockSpec((1, tm, d), lambda b, t, c: (b, t, 0)),
                   pl.BlockSpec((1, 1, hist_rows, ck), lambda b, t, c: (b, t, 0, c))],
        out_shape=[jax.ShapeDtypeStruct((nb, s, d), F32), jax.ShapeDtypeStruct((nb, nt, hist_rows, dff), F32)],
        scratch_shapes=[pltpu.VMEM((tm, d), BF16), pltpu.VMEM((tm, d), F32),
                        pltpu.VMEM((hist_rows + tm, ck), F32), pltpu.VMEM((nck, hist_rows, ck), F32)],
        compiler_params=_cparams(("parallel", "arbitrary", "arbitrary")),
        name="ffn",
    )(x3, g_norm, hist, w_up, w_up, w_conv, b_conv, w_down, g_final)
    return y, hist_all[:, nt - 1]


def _sb_chains(zs, cbs, mo, mask):
    lsps, lons, firsts = [], [], []
    for chain in zs:
        for t, z in enumerate(chain):
            lon = jnp.maximum(z, 0.0) + jnp.log2(1.0 + jnp.exp2(-jnp.abs(z)))
            lsps.append(z - lon)
            lon = lon if (mask is None or t) else jnp.where(mask, lon, 0.0)
            firsts.append(lon[:, :1])
            lons.append(lon.astype(BF16))
    rs = [_dot(lon, mo) for lon in lons]
    out, new_cbs, n = [], [], 0
    for chain, cb in zip(zs, cbs):
        weights = []
        for t, z in enumerate(chain):
            tk = z.shape[1]
            a = jnp.exp2(lsps[n] + rs[n] + jnp.tile(cb, (1, tk // LANES)))
            weights.append((a if (mask is None or t) else jnp.where(mask, a, 0.0)).astype(BF16))
            cb = cb + jnp.broadcast_to(rs[n][:, :1] - firsts[n], cb.shape)
            n += 1
        out.append(weights)
        new_cbs.append(cb)
    return out, new_cbs


SB_TILES_PER_ITER = 4


def _sb_mo(tk):
    j = np.arange(tk)[:, None]
    s = np.arange(tk)[None, :]
    return jnp.asarray(-(j > s).astype(np.float32), BF16)


def _sb_prompt_kernel(bias_ref, q_ref, kt_ref, vt_ref, mo_ref, o_ref, acc_ref, cb_ref, *, tq):
    hp = pl.program_id(1)
    qi = pl.program_id(2)
    q = q_ref[0]
    lane = lax.broadcasted_iota(jnp.int32, q.shape, 1)
    tri = lax.broadcasted_iota(jnp.int32, (tq, tq), 1) < lax.broadcasted_iota(jnp.int32, (tq, tq), 0)
    mo = mo_ref[...]
    qh = [jnp.where(lane < SB_DIM, q, 0), jnp.where(lane >= SB_DIM, q, 0)]
    bias = [bias_ref[2 * hp], bias_ref[2 * hp + 1]]

    def tiles(kbs, mask, first):
        kts = [kt_ref[0, kb] for kb in kbs]
        zs = [[_dot(qh[sub], kt) + bias[sub] for kt in kts] for sub in range(2)]
        cbs = [jnp.zeros((tq, LANES), F32) if first else cb_ref[sub] for sub in range(2)]
        weights, cbs = _sb_chains(zs, cbs, mo, mask)
        vts = [vt_ref[0, kb] for kb in kbs]
        pvs = [[_dot_nt(a, vt) for a, vt in zip(weights[sub], vts)] for sub in range(2)]
        for sub in range(2):
            acc = sum(pvs[sub][1:], pvs[sub][0])
            acc_ref[sub] = acc if first else acc_ref[sub] + acc
            cb_ref[sub] = cbs[sub]

    n_first = qi % SB_TILES_PER_ITER + 1
    for size in range(1, SB_TILES_PER_ITER + 1):
        @pl.when(n_first == size)
        def _(size=size):
            tiles([qi - n for n in range(size)], tri, True)

    def body(i, carry):
        kb = qi - n_first - SB_TILES_PER_ITER * i
        tiles([kb - n for n in range(SB_TILES_PER_ITER)], None, False)
        return carry

    lax.fori_loop(0, qi // SB_TILES_PER_ITER, body, 0)
    o_ref[0] = jnp.where(lane < SB_DIM, acc_ref[0], acc_ref[1]).astype(o_ref.dtype)


def _sb_prompt(q_b, kt_b, vt_b, bias2):
    b, s, hd = q_b.shape
    tq = kt_b.shape[3]
    nt = s // tq
    return pl.pallas_call(
        functools.partial(_sb_prompt_kernel, tq=tq),
        grid=(b, hd // LANES, nt),
        in_specs=[pl.BlockSpec(memory_space=pltpu.SMEM),
                  pl.BlockSpec((1, tq, LANES), lambda i, h, j: (i, j, h)),
                  pl.BlockSpec((1, nt, LANES, tq), lambda i, h, j: (i, 0, h, 0)),
                  pl.BlockSpec((1, nt, LANES, tq), lambda i, h, j: (i, 0, h, 0)),
                  pl.BlockSpec((tq, tq), lambda i, h, j: (0, 0))],
        out_specs=pl.BlockSpec((1, tq, LANES), lambda i, h, j: (i, j, h)),
        out_shape=jax.ShapeDtypeStruct((b, s, hd), BF16),
        scratch_shapes=[pltpu.VMEM((2, tq, LANES), F32), pltpu.VMEM((2, tq, LANES), F32)],
        compiler_params=_cparams(("parallel", "parallel", "arbitrary")),
        name="sb_prompt",
    )(bias2, q_b, kt_b, vt_b, _sb_mo(tq))


SB_PAGES_PER_STEP = 16


def _sb_sample_kernel(pt_ref, qbd_ref, bias_ref, kn_ref, vn_ref, *rest, nq, npp):
    ck_refs, cv_refs = rest[:npp], rest[npp:2 * npp]
    mo_ref, o_ref, acc_ref, cb_ref = rest[2 * npp:]
    j = pl.program_id(1)
    qbd = qbd_ref[0]
    bias = bias_ref[...]
    mo = mo_ref[...]
    nh = SB_HEADS

    @pl.when(j == 0)
    def _():
        z = _dot_nt(qbd, kn_ref[0].astype(BF16)) + bias
        s_idx = lax.broadcasted_iota(jnp.int32, z.shape, 1)
        q_idx = lax.broadcasted_iota(jnp.int32, z.shape, 0) // nh
        w, cbs = _sb_chains([[z]], [jnp.zeros((z.shape[0], LANES), F32)], mo, s_idx < q_idx)
        acc_ref[...] = _dot(w[0][0], vn_ref[0].astype(BF16))
        cb_ref[...] = cbs[0]

    zs = [_dot(qbd, ck_refs[i][0, 0].astype(BF16)) + bias for i in range(npp)]
    w, cbs = _sb_chains([zs], [cb_ref[...]], mo, None)
    pvs = [_dot_nt(w[0][i], cv_refs[i][0, 0].astype(BF16)) for i in range(npp)]
    acc = acc_ref[...] + sum(pvs[1:], pvs[0])
    acc_ref[...] = acc
    cb_ref[...] = cbs[0]

    @pl.when(j == pl.num_programs(1) - 1)
    def _():
        r_h = lax.broadcasted_iota(jnp.int32, acc.shape, 0) % nh
        l_h = lax.broadcasted_iota(jnp.int32, acc.shape, 1) // SB_DIM
        sel = jnp.where(r_h == l_h, acc, 0.0)
        o_ref[0] = jnp.sum(sel.reshape(nq, nh, acc.shape[1]), axis=1)


def _sb_sample(q_b, k_new, v_new, cache_kt, cache_vt, layer, page_table, bias2):
    b, nq, hd = q_b.shape
    nh = SB_HEADS
    n_pages = page_table.shape[1]
    page = cache_kt.shape[3]
    npp = SB_PAGES_PER_STEP if n_pages % SB_PAGES_PER_STEP == 0 else 1
    qs = q_b.reshape(b, nq, nh, SB_DIM)
    eye = jnp.eye(nh, dtype=BF16)
    qbd = (qs[:, :, :, None, :] * eye[None, None, :, :, None]).reshape(b, nq * nh, hd)
    bias_b = jnp.broadcast_to(jnp.tile(bias2, nq)[:, None], (nq * nh, page))
    pad = lambda a: jnp.pad(a, ((0, 0), (0, page - nq), (0, 0)))
    page_spec = lambda i_page: pl.BlockSpec(
        (1, 1, hd, page), lambda i, j, pt: (layer, pt[i, n_pages - 1 - (j * npp + i_page)], 0, 0))
    return pl.pallas_call(
        functools.partial(_sb_sample_kernel, nq=nq, npp=npp),
        grid_spec=pltpu.PrefetchScalarGridSpec(
            num_scalar_prefetch=1,
            grid=(b, n_pages // npp),
            in_specs=([pl.BlockSpec((1, nq * nh, hd), lambda i, j, pt: (i, 0, 0)),
                       pl.BlockSpec((nq * nh, page), lambda i, j, pt: (0, 0)),
                       pl.BlockSpec((1, page, hd), lambda i, j, pt: (i, 0, 0)),
                       pl.BlockSpec((1, page, hd), lambda i, j, pt: (i, 0, 0))]
                      + [page_spec(i) for i in range(npp)] + [page_spec(i) for i in range(npp)]
                      + [pl.BlockSpec((page, page), lambda i, j, pt: (0, 0))]),
            out_specs=pl.BlockSpec((1, nq, hd), lambda i, j, pt: (i, 0, 0)),
            scratch_shapes=[pltpu.VMEM((nq * nh, hd), F32), pltpu.VMEM((nq * nh, LANES), F32)],
        ),
        out_shape=jax.ShapeDtypeStruct((b, nq, hd), F32),
        compiler_params=_cparams(("parallel", "arbitrary")),
        name="sb_sample",
    )(page_table, qbd, bias_b, pad(k_new), pad(v_new), *([cache_kt] * npp), *([cache_vt] * npp), _sb_mo(page))


def _even_layer(x3, g_mix, w_in_p, s5_ops, d_skip, w_glu, b_glu, w_gk_p, b_gk, gla_norm, w_top, w_bot,
                s5_h0, gla_s0, chunk):
    b, s, d = x3.shape
    s5_w = d_skip.shape[1]
    ng = s5_w // S5_GROUP
    x2 = x3.reshape(b * s, d)
    p = _inproj(x2, g_mix, w_in_p)
    p3 = p.reshape(b, s, p.shape[1])
    nblk, _, pb = s5_ops[1].shape
    if s5_h0 is None:
        h0r = h0i = jnp.zeros((b, nblk, pb), F32)
    else:
        h0r, h0i = s5_h0[0].reshape(b, nblk, pb), s5_h0[1].reshape(b, nblk, pb)
    y_conv, hre, him = _s5(p3, s5_ops, h0r, h0i, chunk)
    y_conv = y_conv.reshape(b * s, s5_w)
    fin = lambda a: a.reshape(b, ng, pb * nblk // ng)
    gla_tc = _tile(s, 128) if s >= 16 else 16
    gla_valid = min(s, gla_tc)
    sp = p3 if s >= gla_tc else jnp.pad(p3, ((0, 0), (0, gla_tc - s), (0, 0)))
    s0 = (jnp.zeros((b, GLA_HEADS // 2, 2 * GLA_DK, GLA_DV), F32) if gla_s0 is None
          else gla_s0.reshape(b, GLA_HEADS // 2, 2 * GLA_DK, GLA_DV))
    o_gla, s_fin = _gla(sp, w_gk_p, b_gk, gla_norm, s0, gla_tc, gla_valid)
    o_gla = o_gla[:, :s].reshape(b * s, o_gla.shape[2])
    x2 = _mix_out(x2, y_conv, p, o_gla, d_skip, w_glu, b_glu, w_top, w_bot)
    return x2.reshape(b, s, d), fin(hre), fin(him), s_fin.reshape(b, GLA_HEADS, GLA_DK, GLA_DV)


def kernel(x_prompt, x_sample, state_s5_re, state_s5_im, state_gla, cache_k, cache_v, state_ffn_conv, page_table, norm_mix, norm_ffn, norm_final, w_in0, s5_a_re, s5_a_im, s5_log_dt, s5_b_re, s5_b_im, s5_c_re, s5_c_im, s5_d, s5_w_glu, s5_b_glu, gla_w_gk, gla_b_gk, gla_norm, w_out0, w_qkv1, w_out1, sb_bias, ffn_w_up, ffn_w_conv, ffn_b_conv, ffn_w_down):
    bp, sp, d = x_prompt.shape
    bs, ss, _ = x_sample.shape
    depth = norm_mix.shape[0]
    dff = ffn_w_down.shape[1]
    s5_w = s5_d.shape[1]
    xp, xs = x_prompt, x_sample
    outs_p = {k: [] for k in ("s5r", "s5i", "gla", "k", "v", "conv")}
    outs_s = {k: [] for k in ("s5r", "s5i", "gla", "k", "v", "conv")}
    hist_rows_p = 8
    for l in range(depth):
        i = l // 2
        g_mix = norm_mix[l][None, :]
        if l % 2 == 0:
            in0 = w_in0.shape[2]
            in0p = -(-in0 // LANES) * LANES
            w_in_p = jnp.pad(w_in0[i], ((0, 0), (0, in0p - in0))).astype(BF16)
            w_gk_p = jnp.pad(gla_w_gk[i], ((0, LANES - GLA_RANK), (0, 0))).astype(BF16)
            common = (s5_d[i][None, :], s5_w_glu[i].astype(BF16), s5_b_glu[i][None, :], w_gk_p,
                      gla_b_gk[i][None, :], gla_norm[i][None, :], w_out0[i][:s5_w].astype(BF16),
                      w_out0[i][s5_w:].astype(BF16))
            s5_args = (s5_a_re[i], s5_a_im[i], s5_log_dt[i], s5_b_re[i], s5_b_im[i], s5_c_re[i], s5_c_im[i])
            chunk_p = _tile(sp, S5_CHUNK)
            ops_p = _s5_operators(*s5_args, chunk_p, sp // chunk_p)
            ops_s = _s5_operators(*s5_args, ss, 1)
            xp, hr, hi, sg = _even_layer(xp, g_mix, w_in_p, ops_p, *common, None, None, chunk_p)
            xs, hr2, hi2, sg2 = _even_layer(xs, g_mix, w_in_p, ops_s, *common,
                                            (state_s5_re[i], state_s5_im[i]), state_gla[i], ss)
            for o, vals in ((outs_p, (hr, hi, sg)), (outs_s, (hr2, hi2, sg2))):
                o["s5r"].append(vals[0])
                o["s5i"].append(vals[1])
                o["gla"].append(vals[2])
        else:
            w_qkv = w_qkv1[i].astype(BF16)
            w_out = w_out1[i].astype(BF16)
            bias2 = sb_bias[i] * LOG2E
            qb, kt, vt, ktb, vtb = _qkv_t(xp, g_mix, w_qkv[:, :d], w_qkv[:, d:].T, _tile(sp, 256))
            att = _sb_prompt(qb, ktb, vtb, bias2)
            xp = _proj_res(xp.reshape(bp * sp, d), att.reshape(bp * sp, d), w_out).reshape(bp, sp, d)
            qsb, ks, vs, _, _ = _qkv(xs.reshape(bs * ss, d), g_mix, w_qkv)
            as_t = lambda c: c.transpose(0, 1, 3, 4, 2).reshape(c.shape[0], c.shape[1], d, c.shape[2])
            att_s = _sb_sample(qsb.reshape(bs, ss, d), ks.reshape(bs, ss, d), vs.reshape(bs, ss, d),
                               as_t(cache_k), as_t(cache_v), i, page_table, bias2)
            xs = _proj_res(xs.reshape(bs * ss, d), att_s.reshape(bs * ss, d), w_out).reshape(bs, ss, d)
            outs_p["k"].append(kt.reshape(bp, SB_HEADS, SB_DIM, sp).transpose(0, 3, 1, 2))
            outs_p["v"].append(vt.reshape(bp, SB_HEADS, SB_DIM, sp).transpose(0, 3, 1, 2))
            outs_s["k"].append(ks.reshape(bs, ss, SB_HEADS, SB_DIM))
            outs_s["v"].append(vs.reshape(bs, ss, SB_HEADS, SB_DIM))
        final = l == depth - 1
        w_up = ffn_w_up[l].astype(BF16)
        w_down = ffn_w_down[l].astype(BF16)
        ffn_args = (w_up, ffn_w_conv[l], ffn_b_conv[l][None, :], w_down, norm_final[None, :])
        xp, hist_p = _ffn(xp, norm_ffn[l][None, :], jnp.zeros((bp, hist_rows_p, dff), F32), *ffn_args, 1, final)
        outs_p["conv"].append(hist_p[:, hist_rows_p - (CONV_W - 1):])
        hist_s = state_ffn_conv[l].transpose(1, 0, 2).reshape(1, (CONV_W - 1) * bs, dff)
        xs_t, hist_s = _ffn(xs.transpose(1, 0, 2).reshape(1, ss * bs, d), norm_ffn[l][None, :], hist_s,
                            *ffn_args, bs, final)
        xs = xs_t.reshape(ss, bs, d).transpose(1, 0, 2)
        outs_s["conv"].append(hist_s.reshape(CONV_W - 1, bs, dff).transpose(1, 0, 2))
    st = jnp.stack
    return (xp, xs,
            st(outs_p["s5r"]), st(outs_p["s5i"]), st(outs_p["gla"]), st(outs_p["k"]), st(outs_p["v"]), st(outs_p["conv"]),
            st(outs_s["s5r"]), st(outs_s["s5i"]), st(outs_s["gla"]), st(outs_s["k"]), st(outs_s["v"]), st(outs_s["conv"]))
```
